```python
import math
import jax, jax.numpy as jnp
from jax import lax
import numpy as np

D_MODEL = 2048
BATCH = 4
SEQ = 2048
DEPTH = 1

D_MIX = D_MODEL
SSM_D_INNER = D_MIX // 2
SSM_HEAD_DIM = 64
SSM_N_HEADS = SSM_D_INNER // SSM_HEAD_DIM
SSM_N_GROUPS = 4
SSM_D_STATE = 128
SSM_CONV = 4
SSM_CHUNK = 128
SSM_CONV_DIM = SSM_D_INNER + 2 * SSM_N_GROUPS * SSM_D_STATE
ATTN_WIDTH = D_MIX - SSM_D_INNER
ATTN_HEAD_DIM = 64
ATTN_N_HEADS = ATTN_WIDTH // ATTN_HEAD_DIM
ATTN_N_KV = 2
WINDOW = 128
ATTN_BLOCK = WINDOW
D_FF = 4 * D_MODEL
EPS = 1e-5

IN_PROJ_DIM = SSM_D_INNER + SSM_CONV_DIM + SSM_N_HEADS + ATTN_WIDTH + 2 * ATTN_N_KV * ATTN_HEAD_DIM

kernel_name = 'hymba_ssd_swa_sink_hybrid'


def rmsnorm(x, g):
    xf = x.astype(jnp.float32)
    y = xf * lax.rsqrt(jnp.mean(xf * xf, axis=-1, keepdims=True) + EPS)
    return (y * g.astype(jnp.float32)).astype(x.dtype)


def causal_dwconv(u, w, b):
    K, C = w.shape
    out = lax.conv_general_dilated(
        u, w[:, None, :].astype(u.dtype), window_strides=(1,), padding=[(K - 1, 0)],
        dimension_numbers=('NWC', 'WIO', 'NWC'), feature_group_count=C)
    return out + b.astype(u.dtype)


def ssd_chunked(x, dt, A, Bm, Cm):
    b, L, H, P = x.shape
    G, N = Bm.shape[-2:]
    R = H // G
    Q = SSM_CHUNK
    nc = L // Q
    f32 = jnp.float32
    Xd = (x.astype(f32) * dt[..., None]).reshape(b, nc, Q, G, R, P)
    a = jnp.moveaxis((dt * A).reshape(b, nc, Q, G, R), 2, -1)
    a_cs = jnp.cumsum(a, axis=-1)
    Bc = Bm.astype(f32).reshape(b, nc, Q, G, N)
    Cc = Cm.astype(f32).reshape(b, nc, Q, G, N)
    idx = jnp.arange(Q)
    causal = idx[:, None] >= idx[None, :]
    seg = a_cs[..., :, None] - a_cs[..., None, :]
    Lmat = jnp.exp(jnp.where(causal, seg, -jnp.inf))
    CB = jnp.einsum('bclgn,bcsgn->bcgls', Cc, Bc)
    y_diag = jnp.einsum('bcgls,bcgrls,bcsgrp->bclgrp', CB, Lmat, Xd)
    decay_to_end = jnp.exp(a_cs[..., -1:] - a_cs)
    chunk_states = jnp.einsum('bcsgn,bcgrs,bcsgrp->bcgrpn', Bc, decay_to_end, Xd)
    chunk_decay = jnp.exp(a_cs[..., -1])

    def step(h, inp):
        s_c, d_c = inp
        return d_c[..., None, None] * h + s_c, h

    h0 = jnp.zeros((b, G, R, P, N), f32)
    _, prev = lax.scan(step, h0, (jnp.moveaxis(chunk_states, 1, 0), jnp.moveaxis(chunk_decay, 1, 0)))
    prev = jnp.moveaxis(prev, 0, 1)
    y_off = jnp.einsum('bclgn,bcgrpn,bcgrl->bclgrp', Cc, prev, jnp.exp(a_cs))
    return (y_diag + y_off).reshape(b, L, H, P)


def ssd_mixer(z, xBC, dt_raw, conv_w, conv_b, dt_bias, A_log, D_skip, norm_g):
    b, L, _ = z.shape
    f32 = jnp.float32
    xBC = jax.nn.silu(causal_dwconv(xBC, conv_w, conv_b))
    GN = SSM_N_GROUPS * SSM_D_STATE
    xs = xBC[..., :SSM_D_INNER].reshape(b, L, SSM_N_HEADS, SSM_HEAD_DIM)
    Bm = xBC[..., SSM_D_INNER:SSM_D_INNER + GN].reshape(b, L, SSM_N_GROUPS, SSM_D_STATE)
    Cm = xBC[..., SSM_D_INNER + GN:].reshape(b, L, SSM_N_GROUPS, SSM_D_STATE)
    dt = jax.nn.softplus(dt_raw.astype(f32) + dt_bias.astype(f32))
    A = -jnp.exp(A_log.astype(f32))
    y = ssd_chunked(xs, dt, A, Bm, Cm)
    y = y + xs.astype(f32) * D_skip.astype(f32)[:, None]
    y = y.reshape(b, L, SSM_D_INNER) * jax.nn.silu(z.astype(f32))
    yg = y.reshape(b, L, SSM_N_GROUPS, SSM_D_INNER // SSM_N_GROUPS)
    yg = yg * lax.rsqrt(jnp.mean(yg * yg, axis=-1, keepdims=True) + EPS)
    y = yg.reshape(b, L, SSM_D_INNER) * norm_g.astype(f32)
    return y.astype(z.dtype)


def swa_sink_attention(q, k, v, sinks):
    b, L, Hq, Dh = q.shape
    Hkv = k.shape[2]
    R = Hq // Hkv
    Qb = ATTN_BLOCK
    nb = L // Qb
    f32 = jnp.float32
    qb = q.astype(f32).reshape(b, nb, Qb, Hkv, R, Dh)
    pad = jnp.zeros((b, Qb, Hkv, Dh), f32)
    kp = jnp.concatenate([pad, k.astype(f32)], axis=1).reshape(b, nb + 1, Qb, Hkv, Dh)
    vp = jnp.concatenate([pad, v.astype(f32)], axis=1).reshape(b, nb + 1, Qb, Hkv, Dh)
    kb = jnp.concatenate([kp[:, :-1], kp[:, 1:]], axis=2)
    vb = jnp.concatenate([vp[:, :-1], vp[:, 1:]], axis=2)
    s = jnp.einsum('bnqhrd,bnkhd->bnhrqk', qb, kb) * (Dh ** -0.5)
    i = jnp.arange(Qb)[:, None]
    j = jnp.arange(2 * Qb)[None, :]
    n = jnp.arange(nb)[:, None, None]
    diff = Qb + i - j
    valid = (diff >= 0) & (diff < WINDOW) & ((n - 1) * Qb + j >= 0)
    s = jnp.where(valid[None, :, None, None], s, -jnp.inf)
    sink = sinks.astype(f32).reshape(Hkv, R)[None, None, :, :, None, None]
    m = jnp.maximum(jnp.max(s, axis=-1, keepdims=True), sink)
    p = jnp.exp(s - m)
    denom = jnp.sum(p, axis=-1, keepdims=True) + jnp.exp(sink - m)
    o = jnp.einsum('bnhrqk,bnkhd->bnqhrd', p / denom, vb)
    return o.reshape(b, L, Hq * Dh).astype(q.dtype)


def setup_inputs(seed: int = 0) -> dict:
    key = jax.random.key(seed)
    ks = jax.random.split(key, 16)
    f32 = jnp.float32

    def nrm(k, shape, scale):
        return jax.random.normal(k, shape, f32) * scale

    x = nrm(ks[0], (BATCH, SEQ, D_MODEL), 1.0)
    mix_norm_g = 1.0 + nrm(ks[1], (DEPTH, D_MODEL), 0.02)
    w_in = nrm(ks[2], (DEPTH, D_MODEL, IN_PROJ_DIM), D_MODEL ** -0.5)
    conv_w = nrm(ks[3], (DEPTH, SSM_CONV, SSM_CONV_DIM), SSM_CONV ** -0.5)
    conv_b = nrm(ks[4], (DEPTH, SSM_CONV_DIM), 0.01)
    dt0 = jnp.exp(jax.random.uniform(ks[5], (DEPTH, SSM_N_HEADS), f32, math.log(1e-3), math.log(1e-1)))
    dt_bias = dt0 + jnp.log(-jnp.expm1(-dt0))
    A_log = jnp.log(jax.random.uniform(ks[6], (DEPTH, SSM_N_HEADS), f32, 1.0, 16.0))
    D_skip = 1.0 + nrm(ks[7], (DEPTH, SSM_N_HEADS), 0.02)
    ssm_norm_g = 1.0 + nrm(ks[8], (DEPTH, SSM_D_INNER), 0.02)
    attn_sinks = nrm(ks[9], (DEPTH, ATTN_N_HEADS), 0.5)
    attn_out_norm_g = 1.0 + nrm(ks[10], (DEPTH, ATTN_WIDTH), 0.02)
    w_out = nrm(ks[11], (DEPTH, D_MIX, D_MODEL), D_MIX ** -0.5)
    mlp_norm_g = 1.0 + nrm(ks[12], (DEPTH, D_MODEL), 0.02)
    w_up = nrm(ks[13], (DEPTH, D_MODEL, D_FF), D_MODEL ** -0.5)
    w_down = nrm(ks[14], (DEPTH, D_FF, D_MODEL), D_FF ** -0.5)
    final_norm_g = 1.0 + nrm(ks[15], (D_MODEL,), 0.02)
    return {'x': x, 'mix_norm_g': mix_norm_g, 'w_in': w_in, 'conv_w': conv_w, 'conv_b': conv_b,
            'dt_bias': dt_bias, 'A_log': A_log, 'D_skip': D_skip, 'ssm_norm_g': ssm_norm_g,
            'attn_sinks': attn_sinks, 'attn_out_norm_g': attn_out_norm_g, 'w_out': w_out,
            'mlp_norm_g': mlp_norm_g, 'w_up': w_up, 'w_down': w_down, 'final_norm_g': final_norm_g}


def reference(x, mix_norm_g, w_in, conv_w, conv_b, dt_bias, A_log, D_skip, ssm_norm_g,
              attn_sinks, attn_out_norm_g, w_out, mlp_norm_g, w_up, w_down, final_norm_g):
    b, L, _ = x.shape
    KVW = ATTN_N_KV * ATTN_HEAD_DIM
    splits = [SSM_D_INNER, SSM_D_INNER + SSM_CONV_DIM, SSM_D_INNER + SSM_CONV_DIM + SSM_N_HEADS,
              SSM_D_INNER + SSM_CONV_DIM + SSM_N_HEADS + ATTN_WIDTH,
              SSM_D_INNER + SSM_CONV_DIM + SSM_N_HEADS + ATTN_WIDTH + KVW]
    for l in range(DEPTH):
        h = rmsnorm(x, mix_norm_g[l])
        proj = jnp.einsum('bsd,de->bse', h, w_in[l])
        z, xBC, dt_raw, q, k, v = jnp.split(proj, splits, axis=-1)
        y_ssm = ssd_mixer(z, xBC, dt_raw, conv_w[l], conv_b[l], dt_bias[l], A_log[l],
                          D_skip[l], ssm_norm_g[l])
        y_att = swa_sink_attention(q.reshape(b, L, ATTN_N_HEADS, ATTN_HEAD_DIM),
                                   k.reshape(b, L, ATTN_N_KV, ATTN_HEAD_DIM),
                                   v.reshape(b, L, ATTN_N_KV, ATTN_HEAD_DIM), attn_sinks[l])
        y_att = rmsnorm(y_att, attn_out_norm_g[l])
        y = jnp.concatenate([y_ssm, y_att.astype(y_ssm.dtype)], axis=-1)
        x = x + jnp.einsum('bse,ed->bsd', y, w_out[l])
        h = rmsnorm(x, mlp_norm_g[l])
        u = jnp.einsum('bsd,df->bsf', h, w_up[l])
        x = x + jnp.einsum('bsf,fd->bsd', jnp.square(jax.nn.relu(u)), w_down[l])
    return rmsnorm(x, final_norm_g)
```

```python
import functools

import jax
import jax.numpy as jnp
from jax import lax
from jax.experimental import pallas as pl
from jax.experimental.pallas import tpu as pltpu

F32 = jnp.float32
BF16 = jnp.bfloat16

D_MODEL = 2048
SSM_D_INNER = 1024
SSM_HEAD_DIM = 64
SSM_N_HEADS = 16
SSM_N_GROUPS = 4
SSM_D_STATE = 128
SSM_CONV = 4
SSM_CHUNK = 128
SSM_GN = SSM_N_GROUPS * SSM_D_STATE
SSM_CONV_DIM = SSM_D_INNER + 2 * SSM_GN
SSM_GROUP_WIDTH = SSM_D_INNER // SSM_N_GROUPS
ATTN_WIDTH = 1024
ATTN_HEAD_DIM = 64
ATTN_N_HEADS = 16
ATTN_N_KV = 2
ATTN_KV_WIDTH = 2 * ATTN_N_KV * ATTN_HEAD_DIM
ATTN_BLOCK = 128
D_FF = 8192
EPS = 1e-5

LANES = 128
DT_PAD = LANES
IN_PROJ_PAD = SSM_D_INNER + SSM_CONV_DIM + ATTN_WIDTH + ATTN_KV_WIDTH + DT_PAD
VMEM_LIMIT = 56 * 1024 * 1024

TM_PROJ = 512
TM_MLP = 512
TF_MLP = 1024


def _dot(a, b):
    return jnp.dot(a, b, preferred_element_type=F32)


def _split3(x):
    hi = x.astype(BF16)
    r = x - hi.astype(F32)
    mid = r.astype(BF16)
    lo = (r - mid.astype(F32)).astype(BF16)
    return hi, mid, lo


def _rms_scale(x):
    return lax.rsqrt(jnp.mean(x * x, axis=-1, keepdims=True) + EPS)


def _silu(x):
    return x * (1.0 / (1.0 + jnp.exp(-x)))


def _inproj_kernel(x_ref, g_ref, w_ref, z_ref, xbc_ref, q_ref, kv_ref, dt_ref):
    x = x_ref[...]
    h = (x * _rms_scale(x) * g_ref[...]).astype(BF16)

    def seg(lo, width):
        return _dot(h, w_ref[:, lo:lo + width])

    o = 0
    z_ref[...] = seg(o, SSM_D_INNER).astype(BF16)
    o += SSM_D_INNER
    half = SSM_CONV_DIM // 2
    xbc_ref[:, :half] = seg(o, half).astype(BF16)
    xbc_ref[:, half:] = seg(o + half, half).astype(BF16)
    o += SSM_CONV_DIM
    q_ref[...] = seg(o, ATTN_WIDTH).astype(BF16)
    o += ATTN_WIDTH
    kv_ref[...] = seg(o, ATTN_KV_WIDTH).astype(BF16)
    o += ATTN_KV_WIDTH
    dt_ref[...] = seg(o, DT_PAD)


def _inproj(x2d, g, w):
    t = x2d.shape[0]
    tm = TM_PROJ
    row = lambda width: pl.BlockSpec((tm, width), lambda i: (i, 0))
    return pl.pallas_call(
        _inproj_kernel,
        grid=(t // tm,),
        in_specs=[
            row(D_MODEL),
            pl.BlockSpec((1, D_MODEL), lambda i: (0, 0)),
            pl.BlockSpec((D_MODEL, IN_PROJ_PAD), lambda i: (0, 0),
                         pipeline_mode=pl.Buffered(1)),
        ],
        out_specs=[row(SSM_D_INNER), row(SSM_CONV_DIM), row(ATTN_WIDTH),
                   row(ATTN_KV_WIDTH), row(DT_PAD)],
        out_shape=[
            jax.ShapeDtypeStruct((t, SSM_D_INNER), BF16),
            jax.ShapeDtypeStruct((t, SSM_CONV_DIM), BF16),
            jax.ShapeDtypeStruct((t, ATTN_WIDTH), BF16),
            jax.ShapeDtypeStruct((t, ATTN_KV_WIDTH), BF16),
            jax.ShapeDtypeStruct((t, DT_PAD), F32),
        ],
        compiler_params=pltpu.CompilerParams(
            dimension_semantics=("parallel",), vmem_limit_bytes=VMEM_LIMIT),
        name="inproj",
    )(x2d, g, w)


def _ssd_kernel(xbc_ref, z_ref, dt_ref, cw_ref, cb_ref, dtb_ref, alog_ref,
                dskip_ref, ng_ref, y_ref, cbuf, state):
    Q = SSM_CHUNK
    c = pl.program_id(1)

    @pl.when(c == 0)
    def _():
        cbuf[0:8, :] = jnp.zeros((8, SSM_CONV_DIM), F32)
        state[...] = jnp.zeros_like(state)

    cur = xbc_ref[0].astype(F32)
    cbuf[8:8 + Q, :] = cur
    acc = cb_ref[...] + cw_ref[SSM_CONV - 1:SSM_CONV, :] * cur
    for k in range(SSM_CONV - 1):
        lo = 8 - (SSM_CONV - 1) + k
        acc = acc + cw_ref[k:k + 1, :] * cbuf[lo:lo + Q, :]
    cbuf[0:8, :] = cur[Q - 8:Q, :]
    xc = _silu(acc)
    xs = xc[:, :SSM_D_INNER]

    dtr = dt_ref[0] + dtb_ref[...]
    dt = jnp.maximum(dtr, 0.0) + jnp.log1p(jnp.exp(-jnp.abs(dtr)))
    a = dt * (-jnp.exp(alog_ref[...]))

    row = lax.broadcasted_iota(jnp.int32, (Q, Q), 0)
    col = lax.broadcasted_iota(jnp.int32, (Q, Q), 1)
    causal = row >= col
    tri = causal.astype(BF16)
    a_hi, a_mid, a_lo = _split3(a)
    a_cs = _dot(tri, a_hi) + _dot(tri, a_mid) + _dot(tri, a_lo)
    a_cs_t = a_cs.T

    er = lax.broadcasted_iota(jnp.int32, (LANES, SSM_D_INNER), 0)
    ec = lax.broadcasted_iota(jnp.int32, (LANES, SSM_D_INNER), 1)
    expand = ((ec // SSM_HEAD_DIM) == er).astype(BF16)
    stacked = jnp.concatenate([dt, a_cs], axis=0)
    s_hi, s_mid, s_lo = _split3(stacked)
    ex = _dot(s_hi, expand) + _dot(s_mid, expand) + _dot(s_lo, expand)
    dt_x = ex[:Q]
    acs_x = ex[Q:]
    acs_last = acs_x[Q - 1:Q, :]

    xd = xs * dt_x
    xd16 = xd.astype(BF16)
    xdd16 = (xd * jnp.exp(acs_last - acs_x)).astype(BF16)
    exp_acs = jnp.exp(acs_x)
    chunk_decay = jnp.exp(acs_last)

    lane = lax.broadcasted_iota(jnp.int32, (Q, LANES), 1)
    first_half = lane < SSM_HEAD_DIM
    zero16 = jnp.zeros((Q, LANES), BF16)

    y_parts = []
    for g in range(SSM_N_GROUPS):
        b16 = xc[:, SSM_D_INNER + g * SSM_D_STATE:SSM_D_INNER + (g + 1) * SSM_D_STATE].astype(BF16)
        c16 = xc[:, SSM_D_INNER + SSM_GN + g * SSM_D_STATE:
                 SSM_D_INNER + SSM_GN + (g + 1) * SSM_D_STATE].astype(BF16)
        cb = lax.dot_general(c16, b16, (((1,), (1,)), ((), ())),
                             preferred_element_type=F32)
        ms = []
        for r in range(SSM_N_HEADS // SSM_N_GROUPS):
            h = g * (SSM_N_HEADS // SSM_N_GROUPS) + r
            seg = a_cs[:, h:h + 1] - a_cs_t[h:h + 1, :]
            lmat = jnp.exp(jnp.where(causal, seg, -jnp.inf))
            ms.append((cb * lmat).astype(BF16))
        gl = g * SSM_GROUP_WIDTH
        yd = []
        for pr in range(2):
            xp = xd16[:, gl + pr * LANES:gl + (pr + 1) * LANES]
            rhs = jnp.concatenate([jnp.where(first_half, xp, zero16),
                                   jnp.where(first_half, zero16, xp)], axis=0)
            lhs = jnp.concatenate([ms[2 * pr], ms[2 * pr + 1]], axis=1)
            yd.append(_dot(lhs, rhs))
        y_diag = jnp.concatenate(yd, axis=1)
        st = state[:, gl:gl + SSM_GROUP_WIDTH]
        y_off = _dot(c16, st.astype(BF16)) * exp_acs[:, gl:gl + SSM_GROUP_WIDTH]
        upd = lax.dot_general(b16, xdd16[:, gl:gl + SSM_GROUP_WIDTH],
                              (((0,), (0,)), ((), ())), preferred_element_type=F32)
        state[:, gl:gl + SSM_GROUP_WIDTH] = chunk_decay[:, gl:gl + SSM_GROUP_WIDTH] * st + upd
        y_parts.append(y_diag + y_off)

    y = jnp.concatenate(y_parts, axis=1) + xs * dskip_ref[...]
    y = y * _silu(z_ref[0].astype(F32))
    outs = []
    for g in range(SSM_N_GROUPS):
        yg = y[:, g * SSM_GROUP_WIDTH:(g + 1) * SSM_GROUP_WIDTH]
        outs.append(yg * _rms_scale(yg))
    y_ref[0] = (jnp.concatenate(outs, axis=1) * ng_ref[...]).astype(BF16)


def _ssd(xbc, z, dt, conv_w, conv_b, dt_bias, a_log, d_skip, norm_g):
    b, l, _ = xbc.shape
    nc = l // SSM_CHUNK
    tok = lambda width: pl.BlockSpec((1, SSM_CHUNK, width), lambda i, j: (i, j, 0))
    full = lambda r, width: pl.BlockSpec((r, width), lambda i, j: (0, 0))
    return pl.pallas_call(
        _ssd_kernel,
        grid=(b, nc),
        in_specs=[tok(SSM_CONV_DIM), tok(SSM_D_INNER), tok(DT_PAD),
                  full(SSM_CONV, SSM_CONV_DIM), full(1, SSM_CONV_DIM),
                  full(1, DT_PAD), full(1, DT_PAD),
                  full(1, SSM_D_INNER), full(1, SSM_D_INNER)],
        out_specs=tok(SSM_D_INNER),
        out_shape=jax.ShapeDtypeStruct((b, l, SSM_D_INNER), BF16),
        scratch_shapes=[pltpu.VMEM((8 + SSM_CHUNK, SSM_CONV_DIM), F32),
                        pltpu.VMEM((SSM_D_STATE, SSM_D_INNER), F32)],
        compiler_params=pltpu.CompilerParams(
            dimension_semantics=("parallel", "arbitrary"), vmem_limit_bytes=VMEM_LIMIT),
        name="ssd",
    )(xbc, z, dt, conv_w, conv_b, dt_bias, a_log, d_skip, norm_g)


def _attn_kernel(sink_ref, q_ref, kvp_ref, kvc_ref, g_ref, o_ref):
    QB = ATTN_BLOCK
    D = ATTN_HEAD_DIM
    n = pl.program_id(1)
    kv = jnp.concatenate([kvp_ref[0], kvc_ref[0]], axis=0).astype(F32)
    k01 = kv[:, :2 * D]
    v01 = kv[:, 2 * D:]
    k_t = k01.T
    v01r = pltpu.roll(v01, D, axis=1)

    lane_kv = lax.broadcasted_iota(jnp.int32, (2 * QB, LANES), 1)
    first_kv = lane_kv < D
    ones_e = first_kv.astype(BF16)
    ones_o = 1 - ones_e
    zero_t = jnp.zeros((D, 2 * QB), BF16)

    i = lax.broadcasted_iota(jnp.int32, (QB, 2 * QB), 0)
    j = lax.broadcasted_iota(jnp.int32, (QB, 2 * QB), 1)
    valid = (j > i) & (j <= i + QB) & ((n - 1) * QB + j >= 0)
    lane_q = lax.broadcasted_iota(jnp.int32, (QB, LANES), 1)
    first_q = lane_q < D

    per_kv = ATTN_N_HEADS // ATTN_N_KV // 2
    outs = []
    for g in range(ATTN_N_KV):
        kg_t = k_t[g * D:(g + 1) * D, :].astype(BF16)
        rhs = jnp.concatenate([jnp.concatenate([kg_t, zero_t], axis=1),
                               jnp.concatenate([zero_t, kg_t], axis=1)], axis=0)
        if g == 0:
            v_e = jnp.where(first_kv, v01, 0.0)
            v_o = jnp.where(first_kv, 0.0, v01r)
        else:
            v_e = jnp.where(first_kv, v01r, 0.0)
            v_o = jnp.where(first_kv, 0.0, v01)
        vb = jnp.concatenate(
            [jnp.concatenate([v_e.astype(BF16), ones_e], axis=1),
             jnp.concatenate([v_o.astype(BF16), ones_o], axis=1)], axis=0)
        for pr in range(per_kv):
            jp = g * per_kv + pr
            qp = q_ref[0, :, jp * LANES:(jp + 1) * LANES]
            s = _dot(qp, rhs) * (D ** -0.5)
            s_e = jnp.where(valid, s[:, :2 * QB], -jnp.inf)
            s_o = jnp.where(valid, s[:, 2 * QB:], -jnp.inf)
            sink_e = sink_ref[2 * jp]
            sink_o = sink_ref[2 * jp + 1]
            m_e = jnp.maximum(jnp.max(s_e, axis=-1, keepdims=True), sink_e)
            m_o = jnp.maximum(jnp.max(s_o, axis=-1, keepdims=True), sink_o)
            p = jnp.concatenate([jnp.exp(s_e - m_e), jnp.exp(s_o - m_o)], axis=1).astype(BF16)
            o = _dot(p, vb)
            sink_term = jnp.where(first_q, jnp.exp(sink_e - m_e), jnp.exp(sink_o - m_o))
            outs.append(o[:, :LANES] / (o[:, LANES:] + sink_term))
    y = jnp.concatenate(outs, axis=1)
    o_ref[0] = (y * _rms_scale(y) * g_ref[...]).astype(BF16)


def _attn(q, kv, sinks, norm_g):
    b, l, _ = q.shape
    nb = l // ATTN_BLOCK
    return pl.pallas_call(
        _attn_kernel,
        grid=(b, nb),
        in_specs=[
            pl.BlockSpec(memory_space=pltpu.SMEM),
            pl.BlockSpec((1, ATTN_BLOCK, ATTN_WIDTH), lambda i, j: (i, j, 0)),
            pl.BlockSpec((1, ATTN_BLOCK, ATTN_KV_WIDTH), lambda i, j: (i, jnp.maximum(j - 1, 0), 0)),
            pl.BlockSpec((1, ATTN_BLOCK, ATTN_KV_WIDTH), lambda i, j: (i, j, 0)),
            pl.BlockSpec((1, ATTN_WIDTH), lambda i, j: (0, 0)),
        ],
        out_specs=pl.BlockSpec((1, ATTN_BLOCK, ATTN_WIDTH), lambda i, j: (i, j, 0)),
        out_shape=jax.ShapeDtypeStruct((b, l, ATTN_WIDTH), BF16),
        compiler_params=pltpu.CompilerParams(
            dimension_semantics=("parallel", "parallel"), vmem_limit_bytes=VMEM_LIMIT),
        name="attn",
    )(sinks, q, kv, kv, norm_g)


def _outproj_kernel(x_ref, ys_ref, ya_ref, w_ref, g_ref, x1_ref, h2_ref):
    x1 = (x_ref[...] + _dot(ys_ref[...], w_ref[:SSM_D_INNER, :])
          + _dot(ya_ref[...], w_ref[SSM_D_INNER:, :]))
    x1_ref[...] = x1
    h2_ref[...] = (x1 * _rms_scale(x1) * g_ref[...]).astype(BF16)


def _outproj(x2d, y_ssm, y_att, w, g):
    t = x2d.shape[0]
    tm = TM_PROJ
    row = lambda width: pl.BlockSpec((tm, width), lambda i: (i, 0))
    return pl.pallas_call(
        _outproj_kernel,
        grid=(t // tm,),
        in_specs=[row(D_MODEL), row(SSM_D_INNER), row(ATTN_WIDTH),
                  pl.BlockSpec((D_MODEL, D_MODEL), lambda i: (0, 0),
                               pipeline_mode=pl.Buffered(1)),
                  pl.BlockSpec((1, D_MODEL), lambda i: (0, 0))],
        out_specs=[row(D_MODEL), row(D_MODEL)],
        out_shape=[jax.ShapeDtypeStruct((t, D_MODEL), F32),
                   jax.ShapeDtypeStruct((t, D_MODEL), BF16)],
        compiler_params=pltpu.CompilerParams(
            dimension_semantics=("parallel",), vmem_limit_bytes=VMEM_LIMIT),
        name="outproj",
    )(x2d, y_ssm, y_att, w, g)


def _mlp_kernel(x1_ref, h2_ref, wu_ref, wd_ref, g_ref, o_ref):
    j = pl.program_id(1)

    @pl.when(j == 0)
    def _():
        o_ref[...] = x1_ref[...]

    u = jnp.maximum(_dot(h2_ref[...], wu_ref[...]), 0.0)
    o_ref[...] += _dot((u * u).astype(BF16), wd_ref[...])

    @pl.when(j == pl.num_programs(1) - 1)
    def _():
        x2 = o_ref[...]
        o_ref[...] = x2 * _rms_scale(x2) * g_ref[...]


def _mlp(x1, h2, w_up, w_down, g):
    t = x1.shape[0]
    tm, tf = TM_MLP, TF_MLP
    return pl.pallas_call(
        _mlp_kernel,
        grid=(t // tm, D_FF // tf),
        in_specs=[pl.BlockSpec((tm, D_MODEL), lambda i, j: (i, 0)),
                  pl.BlockSpec((tm, D_MODEL), lambda i, j: (i, 0)),
                  pl.BlockSpec((D_MODEL, tf), lambda i, j: (0, j)),
                  pl.BlockSpec((tf, D_MODEL), lambda i, j: (j, 0)),
                  pl.BlockSpec((1, D_MODEL), lambda i, j: (0, 0))],
        out_specs=pl.BlockSpec((tm, D_MODEL), lambda i, j: (i, 0)),
        out_shape=jax.ShapeDtypeStruct((t, D_MODEL), F32),
        compiler_params=pltpu.CompilerParams(
            dimension_semantics=("parallel", "arbitrary"), vmem_limit_bytes=VMEM_LIMIT),
        name="mlp",
    )(x1, h2, w_up, w_down, g)


def _row(v, width=None):
    v = v.astype(F32).reshape(1, -1)
    if width is not None and v.shape[1] < width:
        v = jnp.pad(v, ((0, 0), (0, width - v.shape[1])))
    return v


def kernel(x, mix_norm_g, w_in, conv_w, conv_b, dt_bias, A_log, D_skip, ssm_norm_g,
           attn_sinks, attn_out_norm_g, w_out, mlp_norm_g, w_up, w_down, final_norm_g):
    b, l, d = x.shape
    assert d == D_MODEL and l % SSM_CHUNK == 0 and (b * l) % TM_PROJ == 0
    assert w_in.shape[0] == 1, "one layer"
    x2d = x.reshape(b * l, d)

    wi = w_in[0]
    o_dt = SSM_D_INNER + SSM_CONV_DIM
    o_q = o_dt + SSM_N_HEADS
    w_perm = jnp.concatenate(
        [wi[:, :o_dt], wi[:, o_q:],
         jnp.pad(wi[:, o_dt:o_q], ((0, 0), (0, DT_PAD - SSM_N_HEADS)))], axis=1).astype(BF16)

    z, xbc, q, kv, dt = _inproj(x2d, _row(mix_norm_g[0]), w_perm)

    y_ssm = _ssd(xbc.reshape(b, l, -1), z.reshape(b, l, -1), dt.reshape(b, l, -1),
                 conv_w[0].astype(F32), _row(conv_b[0]),
                 _row(dt_bias[0], DT_PAD), _row(A_log[0], DT_PAD),
                 _row(jnp.repeat(D_skip[0], SSM_HEAD_DIM)), _row(ssm_norm_g[0]))
    y_att = _attn(q.reshape(b, l, -1), kv.reshape(b, l, -1),
                  attn_sinks[0].astype(F32), _row(attn_out_norm_g[0]))

    x1, h2 = _outproj(x2d, y_ssm.reshape(b * l, -1), y_att.reshape(b * l, -1),
                      w_out[0].astype(BF16), _row(mlp_norm_g[0]))
    out = _mlp(x1, h2, w_up[0].astype(BF16), w_down[0].astype(BF16), _row(final_norm_g))
    return out.reshape(b, l, d)
```

```python
import functools

import jax
import jax.numpy as jnp
from jax import lax
from jax.experimental import pallas as pl
from jax.experimental.pallas import tpu as pltpu

F32 = jnp.float32
BF16 = jnp.bfloat16

D_MODEL = 2048
SSM_D_INNER = 1024
SSM_HEAD_DIM = 64
SSM_N_HEADS = 16
SSM_N_GROUPS = 4
SSM_D_STATE = 128
SSM_CONV = 4
SSM_CHUNK = 128
SSM_GN = SSM_N_GROUPS * SSM_D_STATE
SSM_CONV_DIM = SSM_D_INNER + 2 * SSM_GN
SSM_GROUP_WIDTH = SSM_D_INNER // SSM_N_GROUPS
ATTN_WIDTH = 1024
ATTN_HEAD_DIM = 64
ATTN_N_HEADS = 16
ATTN_N_KV = 2
ATTN_KV_WIDTH = 2 * ATTN_N_KV * ATTN_HEAD_DIM
ATTN_BLOCK = 128
D_FF = 8192
EPS = 1e-5

LANES = 128
DT_PAD = LANES
SUBLANES = 8
VMEM_LIMIT = 56 * 1024 * 1024

TM_PROJ = 512
TM_MLP = 512
TF_MLP = 1024


def _dot(a, b):
    return jnp.dot(a, b, preferred_element_type=F32)


def _split3(x):
    hi = x.astype(BF16)
    r = x - hi.astype(F32)
    mid = r.astype(BF16)
    lo = (r - mid.astype(F32)).astype(BF16)
    return hi, mid, lo


def _rms_scale(x):
    return lax.rsqrt(jnp.mean(x * x, axis=-1, keepdims=True) + EPS)


def _silu(x):
    return x * (1.0 / (1.0 + jnp.exp(-x)))


def _inproj_kernel(x_ref, g_ref, wa_ref, wb_ref, wdt_ref,
                   z_ref, xbc_ref, q_ref, kv_ref, dt_ref):
    x = x_ref[...]
    h = (x * _rms_scale(x) * g_ref[...]).astype(BF16)
    half = SSM_CONV_DIM // 2
    z_ref[...] = _dot(h, wa_ref[:, :SSM_D_INNER]).astype(BF16)
    xbc_ref[:, :half] = _dot(h, wa_ref[:, SSM_D_INNER:SSM_D_INNER + half]).astype(BF16)
    xbc_ref[:, half:] = _dot(h, wa_ref[:, SSM_D_INNER + half:]).astype(BF16)
    q_ref[...] = _dot(h, wb_ref[:, :ATTN_WIDTH]).astype(BF16)
    kv_ref[...] = _dot(h, wb_ref[:, ATTN_WIDTH:]).astype(BF16)
    dt_ref[...] = _dot(h, wdt_ref[...])


def _inproj(x2d, g, wa, wb, wdt):
    t = x2d.shape[0]
    tm = TM_PROJ
    row = lambda width: pl.BlockSpec((tm, width), lambda i: (i, 0))
    resident = lambda width: pl.BlockSpec((D_MODEL, width), lambda i: (0, 0),
                                          pipeline_mode=pl.Buffered(1))
    return pl.pallas_call(
        _inproj_kernel,
        grid=(t // tm,),
        in_specs=[
            row(D_MODEL),
            pl.BlockSpec((1, D_MODEL), lambda i: (0, 0)),
            resident(SSM_D_INNER + SSM_CONV_DIM),
            resident(ATTN_WIDTH + ATTN_KV_WIDTH),
            resident(DT_PAD),
        ],
        out_specs=[row(SSM_D_INNER), row(SSM_CONV_DIM), row(ATTN_WIDTH),
                   row(ATTN_KV_WIDTH), row(DT_PAD)],
        out_shape=[
            jax.ShapeDtypeStruct((t, SSM_D_INNER), BF16),
            jax.ShapeDtypeStruct((t, SSM_CONV_DIM), BF16),
            jax.ShapeDtypeStruct((t, ATTN_WIDTH), BF16),
            jax.ShapeDtypeStruct((t, ATTN_KV_WIDTH), BF16),
            jax.ShapeDtypeStruct((t, DT_PAD), F32),
        ],
        compiler_params=pltpu.CompilerParams(
            dimension_semantics=("parallel",), vmem_limit_bytes=VMEM_LIMIT),
        name="inproj",
    )(x2d, g, wa, wb, wdt)


def _ssd_kernel(xbc_ref, z_ref, dt_ref, cw_ref, cb_ref, dtb_ref, alog_ref,
                dskip_ref, ng_ref, wu_ref, wd_ref, y_ref, wu16_ref, wd16_ref, cbuf, state):
    Q = SSM_CHUNK
    c = pl.program_id(1)

    wu16_ref[...] = wu_ref[...].astype(BF16)
    wd16_ref[...] = wd_ref[...].astype(BF16)

    @pl.when(c == 0)
    def _():
        cbuf[0:SUBLANES, :] = jnp.zeros((SUBLANES, SSM_CONV_DIM), F32)
        state[...] = jnp.zeros_like(state)

    cur = xbc_ref[0].astype(F32)
    cbuf[SUBLANES:SUBLANES + Q, :] = cur
    acc = cb_ref[...] + cw_ref[SSM_CONV - 1:SSM_CONV, :] * cur
    for k in range(SSM_CONV - 1):
        lo = SUBLANES - (SSM_CONV - 1) + k
        acc = acc + cw_ref[k:k + 1, :] * cbuf[lo:lo + Q, :]
    cbuf[0:SUBLANES, :] = cur[Q - SUBLANES:Q, :]
    xc = _silu(acc)
    xs = xc[:, :SSM_D_INNER]

    dtr = dt_ref[0] + dtb_ref[...]
    dt = jnp.maximum(dtr, 0.0) + jnp.log1p(jnp.exp(-jnp.abs(dtr)))
    a = dt * (-jnp.exp(alog_ref[...]))

    row = lax.broadcasted_iota(jnp.int32, (Q, Q), 0)
    col = lax.broadcasted_iota(jnp.int32, (Q, Q), 1)
    causal = row >= col
    tri = causal.astype(BF16)
    a_hi, a_mid, a_lo = _split3(a)
    a_cs = _dot(tri, a_hi) + _dot(tri, a_mid) + _dot(tri, a_lo)
    a_cs_t = a_cs.T

    er = lax.broadcasted_iota(jnp.int32, (LANES, SSM_D_INNER), 0)
    ec = lax.broadcasted_iota(jnp.int32, (LANES, SSM_D_INNER), 1)
    expand = ((ec // SSM_HEAD_DIM) == er).astype(BF16)
    stacked = jnp.concatenate([dt, a_cs], axis=0)
    s_hi, s_mid, s_lo = _split3(stacked)
    ex = _dot(s_hi, expand) + _dot(s_mid, expand) + _dot(s_lo, expand)
    dt_x = ex[:Q]
    acs_x = ex[Q:]
    acs_last = acs_x[Q - 1:Q, :]

    xd = xs * dt_x
    xd16 = xd.astype(BF16)
    xdd16 = (xd * jnp.exp(acs_last - acs_x)).astype(BF16)
    exp_acs = jnp.exp(acs_x)
    chunk_decay = jnp.exp(acs_last)

    lane = lax.broadcasted_iota(jnp.int32, (Q, LANES), 1)
    first_half = lane < SSM_HEAD_DIM
    zero16 = jnp.zeros((Q, LANES), BF16)

    y_parts = []
    for g in range(SSM_N_GROUPS):
        b16 = xc[:, SSM_D_INNER + g * SSM_D_STATE:SSM_D_INNER + (g + 1) * SSM_D_STATE].astype(BF16)
        c16 = xc[:, SSM_D_INNER + SSM_GN + g * SSM_D_STATE:
                 SSM_D_INNER + SSM_GN + (g + 1) * SSM_D_STATE].astype(BF16)
        cb = lax.dot_general(c16, b16, (((1,), (1,)), ((), ())),
                             preferred_element_type=F32)
        ms = []
        for r in range(SSM_N_HEADS // SSM_N_GROUPS):
            h = g * (SSM_N_HEADS // SSM_N_GROUPS) + r
            seg = a_cs[:, h:h + 1] - a_cs_t[h:h + 1, :]
            lmat = jnp.exp(jnp.where(causal, seg, -jnp.inf))
            ms.append((cb * lmat).astype(BF16))
        gl = g * SSM_GROUP_WIDTH
        yd = []
        for pr in range(2):
            xp = xd16[:, gl + pr * LANES:gl + (pr + 1) * LANES]
            rhs = jnp.concatenate([jnp.where(first_half, xp, zero16),
                                   jnp.where(first_half, zero16, xp)], axis=0)
            lhs = jnp.concatenate([ms[2 * pr], ms[2 * pr + 1]], axis=1)
            yd.append(_dot(lhs, rhs))
        y_diag = jnp.concatenate(yd, axis=1)
        st = state[:, gl:gl + SSM_GROUP_WIDTH]
        y_off = _dot(c16, st.astype(BF16)) * exp_acs[:, gl:gl + SSM_GROUP_WIDTH]
        upd = lax.dot_general(b16, xdd16[:, gl:gl + SSM_GROUP_WIDTH],
                              (((0,), (0,)), ((), ())), preferred_element_type=F32)
        state[:, gl:gl + SSM_GROUP_WIDTH] = chunk_decay[:, gl:gl + SSM_GROUP_WIDTH] * st + upd
        y_parts.append(y_diag + y_off)

    y = jnp.concatenate(y_parts, axis=1) + xs * dskip_ref[...]
    y = y * _silu(z_ref[0].astype(F32))
    outs = []
    for g in range(SSM_N_GROUPS):
        yg = y[:, g * SSM_GROUP_WIDTH:(g + 1) * SSM_GROUP_WIDTH]
        outs.append(yg * _rms_scale(yg))
    y_ref[0] = (jnp.concatenate(outs, axis=1) * ng_ref[...]).astype(BF16)


def _ssd(xbc, z, dt, conv_w, conv_b, dt_bias, a_log, d_skip, norm_g, w_up, w_down):
    b, l, _ = xbc.shape
    nc = l // SSM_CHUNK
    steps = b * nc
    tok = lambda width: pl.BlockSpec((1, SSM_CHUNK, width), lambda i, j: (i, j, 0))
    full = lambda r, width: pl.BlockSpec((r, width), lambda i, j: (0, 0))
    rows = lambda r, width: pl.BlockSpec((r, width), lambda i, j: (i * nc + j, 0))
    ru, rd = D_MODEL // steps, D_FF // steps
    return pl.pallas_call(
        _ssd_kernel,
        grid=(b, nc),
        in_specs=[tok(SSM_CONV_DIM), tok(SSM_D_INNER), tok(DT_PAD),
                  full(SSM_CONV, SSM_CONV_DIM), full(1, SSM_CONV_DIM),
                  full(1, DT_PAD), full(1, DT_PAD),
                  full(1, SSM_D_INNER), full(1, SSM_D_INNER),
                  rows(ru, D_FF), rows(rd, D_MODEL)],
        out_specs=[tok(SSM_D_INNER), rows(ru, D_FF), rows(rd, D_MODEL)],
        out_shape=[jax.ShapeDtypeStruct((b, l, SSM_D_INNER), BF16),
                   jax.ShapeDtypeStruct((D_MODEL, D_FF), BF16),
                   jax.ShapeDtypeStruct((D_FF, D_MODEL), BF16)],
        scratch_shapes=[pltpu.VMEM((SUBLANES + SSM_CHUNK, SSM_CONV_DIM), F32),
                        pltpu.VMEM((SSM_D_STATE, SSM_D_INNER), F32)],
        compiler_params=pltpu.CompilerParams(
            dimension_semantics=("arbitrary", "arbitrary"), vmem_limit_bytes=VMEM_LIMIT),
        name="ssd",
    )(xbc, z, dt, conv_w, conv_b, dt_bias, a_log, d_skip, norm_g, w_up, w_down)


def _attn_kernel(sink_ref, q_ref, kvp_ref, kvc_ref, g_ref, wo_ref, o_ref, wo16_ref):
    QB = ATTN_BLOCK
    D = ATTN_HEAD_DIM
    n = pl.program_id(1)
    wo16_ref[...] = wo_ref[...].astype(BF16)
    kv = jnp.concatenate([kvp_ref[0], kvc_ref[0]], axis=0).astype(F32)
    k01 = kv[:, :2 * D]
    v01 = kv[:, 2 * D:]
    k_t = k01.T
    v01r = pltpu.roll(v01, D, axis=1)

    lane_kv = lax.broadcasted_iota(jnp.int32, (2 * QB, LANES), 1)
    first_kv = lane_kv < D
    ones_e = first_kv.astype(BF16)
    ones_o = 1 - ones_e
    zero_t = jnp.zeros((D, 2 * QB), BF16)

    i = lax.broadcasted_iota(jnp.int32, (QB, 2 * QB), 0)
    j = lax.broadcasted_iota(jnp.int32, (QB, 2 * QB), 1)
    valid = (j > i) & (j <= i + QB) & ((n - 1) * QB + j >= 0)
    lane_q = lax.broadcasted_iota(jnp.int32, (QB, LANES), 1)
    first_q = lane_q < D

    per_kv = ATTN_N_HEADS // ATTN_N_KV // 2
    outs = []
    for g in range(ATTN_N_KV):
        kg_t = k_t[g * D:(g + 1) * D, :].astype(BF16)
        rhs = jnp.concatenate([jnp.concatenate([kg_t, zero_t], axis=1),
                               jnp.concatenate([zero_t, kg_t], axis=1)], axis=0)
        if g == 0:
            v_e = jnp.where(first_kv, v01, 0.0)
            v_o = jnp.where(first_kv, 0.0, v01r)
        else:
            v_e = jnp.where(first_kv, v01r, 0.0)
            v_o = jnp.where(first_kv, 0.0, v01)
        vb = jnp.concatenate(
            [jnp.concatenate([v_e.astype(BF16), ones_e], axis=1),
             jnp.concatenate([v_o.astype(BF16), ones_o], axis=1)], axis=0)
        for pr in range(per_kv):
            jp = g * per_kv + pr
            qp = q_ref[0, :, jp * LANES:(jp + 1) * LANES]
            s = _dot(qp, rhs) * (D ** -0.5)
            s_e = jnp.where(valid, s[:, :2 * QB], -jnp.inf)
            s_o = jnp.where(valid, s[:, 2 * QB:], -jnp.inf)
            sink_e = sink_ref[2 * jp]
            sink_o = sink_ref[2 * jp + 1]
            m_e = jnp.maximum(jnp.max(s_e, axis=-1, keepdims=True), sink_e)
            m_o = jnp.maximum(jnp.max(s_o, axis=-1, keepdims=True), sink_o)
            p = jnp.concatenate([jnp.exp(s_e - m_e), jnp.exp(s_o - m_o)], axis=1).astype(BF16)
            o = _dot(p, vb)
            sink_term = jnp.where(first_q, jnp.exp(sink_e - m_e), jnp.exp(sink_o - m_o))
            outs.append(o[:, :LANES] / (o[:, LANES:] + sink_term))
    y = jnp.concatenate(outs, axis=1)
    o_ref[0] = (y * _rms_scale(y) * g_ref[...]).astype(BF16)


def _attn(q, kv, sinks, norm_g, w_out):
    b, l, _ = q.shape
    nb = l // ATTN_BLOCK
    ro = D_MODEL // (b * nb)
    wrows = pl.BlockSpec((ro, D_MODEL), lambda i, j: (i * nb + j, 0))
    return pl.pallas_call(
        _attn_kernel,
        grid=(b, nb),
        in_specs=[
            pl.BlockSpec(memory_space=pltpu.SMEM),
            pl.BlockSpec((1, ATTN_BLOCK, ATTN_WIDTH), lambda i, j: (i, j, 0)),
            pl.BlockSpec((1, ATTN_BLOCK, ATTN_KV_WIDTH), lambda i, j: (i, jnp.maximum(j - 1, 0), 0)),
            pl.BlockSpec((1, ATTN_BLOCK, ATTN_KV_WIDTH), lambda i, j: (i, j, 0)),
            pl.BlockSpec((1, ATTN_WIDTH), lambda i, j: (0, 0)),
            wrows,
        ],
        out_specs=[pl.BlockSpec((1, ATTN_BLOCK, ATTN_WIDTH), lambda i, j: (i, j, 0)), wrows],
        out_shape=[jax.ShapeDtypeStruct((b, l, ATTN_WIDTH), BF16),
                   jax.ShapeDtypeStruct((D_MODEL, D_MODEL), BF16)],
        compiler_params=pltpu.CompilerParams(
            dimension_semantics=("parallel", "parallel"), vmem_limit_bytes=VMEM_LIMIT),
        name="attn",
    )(sinks, q, kv, kv, norm_g, w_out)


def _outproj_kernel(x_ref, ys_ref, ya_ref, w_ref, g_ref, x1_ref, h2_ref):
    x1 = (x_ref[...] + _dot(ys_ref[...], w_ref[:SSM_D_INNER, :])
          + _dot(ya_ref[...], w_ref[SSM_D_INNER:, :]))
    x1_ref[...] = x1
    h2_ref[...] = (x1 * _rms_scale(x1) * g_ref[...]).astype(BF16)


def _outproj(x2d, y_ssm, y_att, w, g):
    t = x2d.shape[0]
    tm = TM_PROJ
    row = lambda width: pl.BlockSpec((tm, width), lambda i: (i, 0))
    return pl.pallas_call(
        _outproj_kernel,
        grid=(t // tm,),
        in_specs=[row(D_MODEL), row(SSM_D_INNER), row(ATTN_WIDTH),
                  pl.BlockSpec((D_MODEL, D_MODEL), lambda i: (0, 0),
                               pipeline_mode=pl.Buffered(1)),
                  pl.BlockSpec((1, D_MODEL), lambda i: (0, 0))],
        out_specs=[row(D_MODEL), row(D_MODEL)],
        out_shape=[jax.ShapeDtypeStruct((t, D_MODEL), F32),
                   jax.ShapeDtypeStruct((t, D_MODEL), BF16)],
        compiler_params=pltpu.CompilerParams(
            dimension_semantics=("parallel",), vmem_limit_bytes=VMEM_LIMIT),
        name="outproj",
    )(x2d, y_ssm, y_att, w, g)


def _mlp_kernel(x1_ref, h2_ref, wu_ref, wd_ref, g_ref, o_ref):
    j = pl.program_id(1)

    @pl.when(j == 0)
    def _():
        o_ref[...] = x1_ref[...]

    u = jnp.maximum(_dot(h2_ref[...], wu_ref[...]), 0.0)
    o_ref[...] += _dot((u * u).astype(BF16), wd_ref[...])

    @pl.when(j == pl.num_programs(1) - 1)
    def _():
        x2 = o_ref[...]
        o_ref[...] = x2 * _rms_scale(x2) * g_ref[...]


def _mlp(x1, h2, w_up, w_down, g):
    t = x1.shape[0]
    tm, tf = TM_MLP, TF_MLP
    return pl.pallas_call(
        _mlp_kernel,
        grid=(t // tm, D_FF // tf),
        in_specs=[pl.BlockSpec((tm, D_MODEL), lambda i, j: (i, 0)),
                  pl.BlockSpec((tm, D_MODEL), lambda i, j: (i, 0)),
                  pl.BlockSpec((D_MODEL, tf), lambda i, j: (0, j)),
                  pl.BlockSpec((tf, D_MODEL), lambda i, j: (j, 0)),
                  pl.BlockSpec((1, D_MODEL), lambda i, j: (0, 0))],
        out_specs=pl.BlockSpec((tm, D_MODEL), lambda i, j: (i, 0)),
        out_shape=jax.ShapeDtypeStruct((t, D_MODEL), F32),
        compiler_params=pltpu.CompilerParams(
            dimension_semantics=("parallel", "arbitrary"), vmem_limit_bytes=VMEM_LIMIT),
        name="mlp",
    )(x1, h2, w_up, w_down, g)


def _row(v, width=None):
    v = v.astype(F32).reshape(1, -1)
    if width is not None and v.shape[1] < width:
        v = jnp.pad(v, ((0, 0), (0, width - v.shape[1])))
    return v


def kernel(x, mix_norm_g, w_in, conv_w, conv_b, dt_bias, A_log, D_skip, ssm_norm_g,
           attn_sinks, attn_out_norm_g, w_out, mlp_norm_g, w_up, w_down, final_norm_g):
    b, l, d = x.shape
    assert d == D_MODEL and l % SSM_CHUNK == 0 and (b * l) % TM_PROJ == 0
    assert w_in.shape[0] == 1, "one layer"
    x2d = x.reshape(b * l, d)

    wi = w_in[0]
    o_dt = SSM_D_INNER + SSM_CONV_DIM
    o_q = o_dt + SSM_N_HEADS
    wa = wi[:, :o_dt].astype(BF16)
    wb = wi[:, o_q:].astype(BF16)
    wdt = jnp.pad(wi[:, o_dt:o_q], ((0, 0), (0, DT_PAD - SSM_N_HEADS))).astype(BF16)

    z, xbc, q, kv, dt = _inproj(x2d, _row(mix_norm_g[0]), wa, wb, wdt)

    y_ssm, w_up16, w_down16 = _ssd(
        xbc.reshape(b, l, -1), z.reshape(b, l, -1), dt.reshape(b, l, -1),
        conv_w[0].astype(F32), _row(conv_b[0]),
        _row(dt_bias[0], DT_PAD), _row(A_log[0], DT_PAD),
        _row(jnp.repeat(D_skip[0], SSM_HEAD_DIM)), _row(ssm_norm_g[0]),
        w_up[0], w_down[0])
    y_att, w_out16 = _attn(q.reshape(b, l, -1), kv.reshape(b, l, -1),
                           attn_sinks[0].astype(F32), _row(attn_out_norm_g[0]), w_out[0])

    x1, h2 = _outproj(x2d, y_ssm.reshape(b * l, -1), y_att.reshape(b * l, -1),
                      w_out16, _row(mlp_norm_g[0]))
    out = _mlp(x1, h2, w_up16, w_down16, _row(final_norm_g))
    return out.reshape(b, l, d)
```

```python
import functools

import jax
import jax.numpy as jnp
from jax import lax
from jax.experimental import pallas as pl
from jax.experimental.pallas import tpu as pltpu

F32 = jnp.float32
BF16 = jnp.bfloat16

D_MODEL = 2048
SSM_D_INNER = 1024
SSM_HEAD_DIM = 64
SSM_N_HEADS = 16
SSM_N_GROUPS = 4
SSM_D_STATE = 128
SSM_CONV = 4
SSM_CHUNK = 128
SSM_GN = SSM_N_GROUPS * SSM_D_STATE
SSM_CONV_DIM = SSM_D_INNER + 2 * SSM_GN
SSM_GROUP_WIDTH = SSM_D_INNER // SSM_N_GROUPS
ATTN_WIDTH = 1024
ATTN_HEAD_DIM = 64
ATTN_N_HEADS = 16
ATTN_N_KV = 2
ATTN_KV_WIDTH = 2 * ATTN_N_KV * ATTN_HEAD_DIM
ATTN_BLOCK = 128
D_FF = 8192
EPS = 1e-5

LANES = 128
DT_PAD = LANES
SUBLANES = 8
MXU_WIDTH = 256
ATTN_STAGES_PER_PROJ_STAGE = 4
VMEM_LIMIT = 56 * 1024 * 1024

TM_PROJ = 512
TM_MLP = 512
TF_MLP = 1024


def _dot(a, b):
    return jnp.dot(a, b, preferred_element_type=F32)


def _split3(x):
    hi = x.astype(BF16)
    r = x - hi.astype(F32)
    mid = r.astype(BF16)
    lo = (r - mid.astype(F32)).astype(BF16)
    return hi, mid, lo


def _rms_scale(x):
    return lax.rsqrt(jnp.mean(x * x, axis=-1, keepdims=True) + EPS)


def _silu(x):
    return x * (1.0 / (1.0 + jnp.exp(-x)))


def _interleave(major, minor, ratio):
    major_live = minor_live = True
    while major_live or minor_live:
        if major_live:
            major_live = next(major, _DONE) is not _DONE
        for _ in range(ratio):
            if minor_live:
                minor_live = next(minor, _DONE) is not _DONE


_DONE = object()


def _attn_block_stages(q_blk, kv_prev, kv_cur, first_block, sink_ref, emit):
    QB = ATTN_BLOCK
    D = ATTN_HEAD_DIM
    kv = jnp.concatenate([kv_prev, kv_cur], axis=0).astype(F32)
    k01 = kv[:, :2 * D]
    v01 = kv[:, 2 * D:]
    k_t = k01.T
    v01r = pltpu.roll(v01, D, axis=1)

    lane_kv = lax.broadcasted_iota(jnp.int32, (2 * QB, LANES), 1)
    first_kv = lane_kv < D
    ones_e = first_kv.astype(BF16)
    ones_o = 1 - ones_e
    zero_t = jnp.zeros((D, 2 * QB), BF16)

    i = lax.broadcasted_iota(jnp.int32, (QB, 2 * QB), 0)
    j = lax.broadcasted_iota(jnp.int32, (QB, 2 * QB), 1)
    valid = (j > i) & (j <= i + QB) & ((j >= QB) | jnp.logical_not(first_block))
    lane_q = lax.broadcasted_iota(jnp.int32, (QB, LANES), 1)
    first_q = lane_q < D

    per_kv = ATTN_N_HEADS // ATTN_N_KV // 2
    rhs, vb = [], []
    for g in range(ATTN_N_KV):
        kg_t = k_t[g * D:(g + 1) * D, :].astype(BF16)
        rhs.append(jnp.concatenate([jnp.concatenate([kg_t, zero_t], axis=1),
                                    jnp.concatenate([zero_t, kg_t], axis=1)], axis=0))
        if g == 0:
            v_e = jnp.where(first_kv, v01, 0.0)
            v_o = jnp.where(first_kv, 0.0, v01r)
        else:
            v_e = jnp.where(first_kv, v01r, 0.0)
            v_o = jnp.where(first_kv, 0.0, v01)
        vb.append(jnp.concatenate(
            [jnp.concatenate([v_e.astype(BF16), ones_e], axis=1),
             jnp.concatenate([v_o.astype(BF16), ones_o], axis=1)], axis=0))
    yield

    def scores(jp):
        s = _dot(q_blk(jp), rhs[jp // per_kv]) * (D ** -0.5)
        s_e = jnp.where(valid, s[:, :2 * QB], -jnp.inf)
        s_o = jnp.where(valid, s[:, 2 * QB:], -jnp.inf)
        sink_e = sink_ref[2 * jp]
        sink_o = sink_ref[2 * jp + 1]
        m_e = jnp.maximum(jnp.max(s_e, axis=-1, keepdims=True), sink_e)
        m_o = jnp.maximum(jnp.max(s_o, axis=-1, keepdims=True), sink_o)
        p = jnp.concatenate([jnp.exp(s_e - m_e), jnp.exp(s_o - m_o)], axis=1).astype(BF16)
        sink_term = jnp.where(first_q, jnp.exp(sink_e - m_e), jnp.exp(sink_o - m_o))
        return p, sink_term

    def values(jp, p, sink_term):
        o = _dot(p, vb[jp // per_kv])
        return o[:, :LANES] / (o[:, LANES:] + sink_term)

    n_pairs = ATTN_N_KV * per_kv
    outs = []
    pending = scores(0)
    yield
    for jp in range(1, n_pairs):
        nxt = scores(jp)
        yield
        outs.append(values(jp - 1, *pending))
        pending = nxt
        yield
    outs.append(values(n_pairs - 1, *pending))
    emit(jnp.concatenate(outs, axis=1))
    yield


def _inproj_attn_kernel(sink_ref, x_ref, g_ref, wa_ref, wb_ref, wdt_ref, ng_ref,
                        z_ref, xbc_ref, dt_ref, yatt_ref, cur_scr, prev_scr, kvtail_scr,
                        *, tiles_per_seq):
    s = pl.program_id(0)
    n_tiles = pl.num_programs(0) - 1
    blocks = TM_PROJ // ATTN_BLOCK
    tail = slice(TM_PROJ - ATTN_BLOCK, TM_PROJ)

    def project():
        x = x_ref[...]
        h = (x * _rms_scale(x) * g_ref[...]).astype(BF16)
        yield
        for w_ref, dsts in ((wa_ref, ((z_ref, 0, SSM_D_INNER), (xbc_ref, SSM_D_INNER, SSM_CONV_DIM))),
                            (wb_ref, ((cur_scr, 0, ATTN_WIDTH + ATTN_KV_WIDTH),))):
            for dst, base, width in dsts:
                for lo in range(0, width, MXU_WIDTH):
                    dst[:, lo:lo + MXU_WIDTH] = _dot(
                        h, w_ref[:, base + lo:base + lo + MXU_WIDTH]).astype(BF16)
                    yield
        dt_ref[...] = _dot(h, wdt_ref[...])
        yield

    def rotate_buffers():
        kvtail_scr[...] = prev_scr[tail, ATTN_WIDTH:]
        prev_scr[...] = cur_scr[...]

    def attend():
        seq_start = ((s - 1) % tiles_per_seq) == 0
        for r in range(blocks):
            lo = r * ATTN_BLOCK
            kv_cur = prev_scr[lo:lo + ATTN_BLOCK, ATTN_WIDTH:]
            if r == 0:
                kv_prev = kvtail_scr[...]
                first_block = seq_start
            else:
                kv_prev = prev_scr[lo - ATTN_BLOCK:lo, ATTN_WIDTH:]
                first_block = False
            q_blk = lambda jp, lo=lo: prev_scr[lo:lo + ATTN_BLOCK, jp * LANES:(jp + 1) * LANES]

            def emit(y, lo=lo):
                yatt_ref[lo:lo + ATTN_BLOCK, :] = (y * _rms_scale(y) * ng_ref[...]).astype(BF16)

            yield from _attn_block_stages(q_blk, kv_prev, kv_cur, first_block, sink_ref, emit)

    @pl.when(s == 0)
    def _():
        prev_scr[tail, ATTN_WIDTH:] = jnp.zeros((ATTN_BLOCK, ATTN_KV_WIDTH), BF16)
        for _ in project():
            pass

    @pl.when((s > 0) & (s < n_tiles))
    def _():
        rotate_buffers()
        _interleave(project(), attend(), ratio=ATTN_STAGES_PER_PROJ_STAGE)

    @pl.when(s == n_tiles)
    def _():
        rotate_buffers()
        for _ in attend():
            pass


def _inproj_attn(x2d, g, wa, wb, wdt, sinks, attn_g, seq_len):
    t = x2d.shape[0]
    tm = TM_PROJ
    n_tiles = t // tm
    cur = lambda width: pl.BlockSpec((tm, width), lambda i: (jnp.minimum(i, n_tiles - 1), 0))
    prev = lambda width: pl.BlockSpec((tm, width), lambda i: (jnp.maximum(i - 1, 0), 0))
    resident = lambda width: pl.BlockSpec((D_MODEL, width), lambda i: (0, 0),
                                          pipeline_mode=pl.Buffered(1))
    vec = lambda width: pl.BlockSpec((1, width), lambda i: (0, 0))
    return pl.pallas_call(
        functools.partial(_inproj_attn_kernel, tiles_per_seq=seq_len // tm),
        grid=(n_tiles + 1,),
        in_specs=[
            pl.BlockSpec(memory_space=pltpu.SMEM),
            cur(D_MODEL),
            vec(D_MODEL),
            resident(SSM_D_INNER + SSM_CONV_DIM),
            resident(ATTN_WIDTH + ATTN_KV_WIDTH),
            resident(DT_PAD),
            vec(ATTN_WIDTH),
        ],
        out_specs=[cur(SSM_D_INNER), cur(SSM_CONV_DIM), cur(DT_PAD), prev(ATTN_WIDTH)],
        out_shape=[
            jax.ShapeDtypeStruct((t, SSM_D_INNER), BF16),
            jax.ShapeDtypeStruct((t, SSM_CONV_DIM), BF16),
            jax.ShapeDtypeStruct((t, DT_PAD), F32),
            jax.ShapeDtypeStruct((t, ATTN_WIDTH), BF16),
        ],
        scratch_shapes=[pltpu.VMEM((tm, ATTN_WIDTH + ATTN_KV_WIDTH), BF16),
                        pltpu.VMEM((tm, ATTN_WIDTH + ATTN_KV_WIDTH), BF16),
                        pltpu.VMEM((ATTN_BLOCK, ATTN_KV_WIDTH), BF16)],
        compiler_params=pltpu.CompilerParams(
            dimension_semantics=("arbitrary",), vmem_limit_bytes=VMEM_LIMIT),
        name="inproj_attn",
    )(sinks, x2d, g, wa, wb, wdt, attn_g)


def _ssd_kernel(xbc_ref, z_ref, dt_ref, cw_ref, cb_ref, dtb_ref, alog_ref,
                dskip_ref, ng_ref, wu_ref, wd_ref, wo_ref,
                y_ref, wu16_ref, wd16_ref, wo16_ref, cbuf, state):
    Q = SSM_CHUNK
    c = pl.program_id(1)

    wu16_ref[...] = wu_ref[...].astype(BF16)
    wd16_ref[...] = wd_ref[...].astype(BF16)
    wo16_ref[...] = wo_ref[...].astype(BF16)

    @pl.when(c == 0)
    def _():
        cbuf[0:SUBLANES, :] = jnp.zeros((SUBLANES, SSM_CONV_DIM), F32)
        state[...] = jnp.zeros_like(state)

    cur = xbc_ref[0].astype(F32)
    cbuf[SUBLANES:SUBLANES + Q, :] = cur
    acc = cb_ref[...] + cw_ref[SSM_CONV - 1:SSM_CONV, :] * cur
    for k in range(SSM_CONV - 1):
        lo = SUBLANES - (SSM_CONV - 1) + k
        acc = acc + cw_ref[k:k + 1, :] * cbuf[lo:lo + Q, :]
    cbuf[0:SUBLANES, :] = cur[Q - SUBLANES:Q, :]
    xc = _silu(acc)
    xs = xc[:, :SSM_D_INNER]

    dtr = dt_ref[0] + dtb_ref[...]
    dt = jnp.maximum(dtr, 0.0) + jnp.log1p(jnp.exp(-jnp.abs(dtr)))
    a = dt * (-jnp.exp(alog_ref[...]))

    row = lax.broadcasted_iota(jnp.int32, (Q, Q), 0)
    col = lax.broadcasted_iota(jnp.int32, (Q, Q), 1)
    causal = row >= col
    tri = causal.astype(BF16)
    a_hi, a_mid, a_lo = _split3(a)
    a_cs = _dot(tri, a_hi) + _dot(tri, a_mid) + _dot(tri, a_lo)
    a_cs_t = a_cs.T

    er = lax.broadcasted_iota(jnp.int32, (LANES, SSM_D_INNER), 0)
    ec = lax.broadcasted_iota(jnp.int32, (LANES, SSM_D_INNER), 1)
    expand = ((ec // SSM_HEAD_DIM) == er).astype(BF16)
    stacked = jnp.concatenate([dt, a_cs], axis=0)
    s_hi, s_mid, s_lo = _split3(stacked)
    ex = _dot(s_hi, expand) + _dot(s_mid, expand) + _dot(s_lo, expand)
    dt_x = ex[:Q]
    acs_x = ex[Q:]
    acs_last = acs_x[Q - 1:Q, :]

    xd = xs * dt_x
    xd16 = xd.astype(BF16)
    xdd16 = (xd * jnp.exp(acs_last - acs_x)).astype(BF16)
    exp_acs = jnp.exp(acs_x)
    chunk_decay = jnp.exp(acs_last)

    lane = lax.broadcasted_iota(jnp.int32, (Q, LANES), 1)
    first_half = lane < SSM_HEAD_DIM
    zero16 = jnp.zeros((Q, LANES), BF16)

    y_parts = []
    for g in range(SSM_N_GROUPS):
        b16 = xc[:, SSM_D_INNER + g * SSM_D_STATE:SSM_D_INNER + (g + 1) * SSM_D_STATE].astype(BF16)
        c16 = xc[:, SSM_D_INNER + SSM_GN + g * SSM_D_STATE:
                 SSM_D_INNER + SSM_GN + (g + 1) * SSM_D_STATE].astype(BF16)
        cb = lax.dot_general(c16, b16, (((1,), (1,)), ((), ())),
                             preferred_element_type=F32)
        ms = []
        for r in range(SSM_N_HEADS // SSM_N_GROUPS):
            h = g * (SSM_N_HEADS // SSM_N_GROUPS) + r
            seg = a_cs[:, h:h + 1] - a_cs_t[h:h + 1, :]
            lmat = jnp.exp(jnp.where(causal, seg, -jnp.inf))
            ms.append((cb * lmat).astype(BF16))
        gl = g * SSM_GROUP_WIDTH
        yd = []
        for pr in range(2):
            xp = xd16[:, gl + pr * LANES:gl + (pr + 1) * LANES]
            rhs = jnp.concatenate([jnp.where(first_half, xp, zero16),
                                   jnp.where(first_half, zero16, xp)], axis=0)
            lhs = jnp.concatenate([ms[2 * pr], ms[2 * pr + 1]], axis=1)
            yd.append(_dot(lhs, rhs))
        y_diag = jnp.concatenate(yd, axis=1)
        st = state[:, gl:gl + SSM_GROUP_WIDTH]
        y_off = _dot(c16, st.astype(BF16)) * exp_acs[:, gl:gl + SSM_GROUP_WIDTH]
        upd = lax.dot_general(b16, xdd16[:, gl:gl + SSM_GROUP_WIDTH],
                              (((0,), (0,)), ((), ())), preferred_element_type=F32)
        state[:, gl:gl + SSM_GROUP_WIDTH] = chunk_decay[:, gl:gl + SSM_GROUP_WIDTH] * st + upd
        y_parts.append(y_diag + y_off)

    y = jnp.concatenate(y_parts, axis=1) + xs * dskip_ref[...]
    y = y * _silu(z_ref[0].astype(F32))
    outs = []
    for g in range(SSM_N_GROUPS):
        yg = y[:, g * SSM_GROUP_WIDTH:(g + 1) * SSM_GROUP_WIDTH]
        outs.append(yg * _rms_scale(yg))
    y_ref[0] = (jnp.concatenate(outs, axis=1) * ng_ref[...]).astype(BF16)


def _ssd(xbc, z, dt, conv_w, conv_b, dt_bias, a_log, d_skip, norm_g, w_up, w_down, w_out):
    b, l, _ = xbc.shape
    nc = l // SSM_CHUNK
    steps = b * nc
    tok = lambda width: pl.BlockSpec((1, SSM_CHUNK, width), lambda i, j: (i, j, 0))
    full = lambda r, width: pl.BlockSpec((r, width), lambda i, j: (0, 0))
    rows = lambda r, width: pl.BlockSpec((r, width), lambda i, j: (i * nc + j, 0))
    ru, rd = D_MODEL // steps, D_FF // steps
    return pl.pallas_call(
        _ssd_kernel,
        grid=(b, nc),
        in_specs=[tok(SSM_CONV_DIM), tok(SSM_D_INNER), tok(DT_PAD),
                  full(SSM_CONV, SSM_CONV_DIM), full(1, SSM_CONV_DIM),
                  full(1, DT_PAD), full(1, DT_PAD),
                  full(1, SSM_D_INNER), full(1, SSM_D_INNER),
                  rows(ru, D_FF), rows(rd, D_MODEL), rows(ru, D_MODEL)],
        out_specs=[tok(SSM_D_INNER), rows(ru, D_FF), rows(rd, D_MODEL), rows(ru, D_MODEL)],
        out_shape=[jax.ShapeDtypeStruct((b, l, SSM_D_INNER), BF16),
                   jax.ShapeDtypeStruct((D_MODEL, D_FF), BF16),
                   jax.ShapeDtypeStruct((D_FF, D_MODEL), BF16),
                   jax.ShapeDtypeStruct((D_MODEL, D_MODEL), BF16)],
        scratch_shapes=[pltpu.VMEM((SUBLANES + SSM_CHUNK, SSM_CONV_DIM), F32),
                        pltpu.VMEM((SSM_D_STATE, SSM_D_INNER), F32)],
        compiler_params=pltpu.CompilerParams(
            dimension_semantics=("arbitrary", "arbitrary"), vmem_limit_bytes=VMEM_LIMIT),
        name="ssd",
    )(xbc, z, dt, conv_w, conv_b, dt_bias, a_log, d_skip, norm_g, w_up, w_down, w_out)


def _outproj_kernel(x_ref, ys_ref, ya_ref, w_ref, g_ref, x1_ref, h2_ref):
    x1 = (x_ref[...] + _dot(ys_ref[...], w_ref[:SSM_D_INNER, :])
          + _dot(ya_ref[...], w_ref[SSM_D_INNER:, :]))
    x1_ref[...] = x1
    h2_ref[...] = (x1 * _rms_scale(x1) * g_ref[...]).astype(BF16)


def _outproj(x2d, y_ssm, y_att, w, g):
    t = x2d.shape[0]
    tm = TM_PROJ
    row = lambda width: pl.BlockSpec((tm, width), lambda i: (i, 0))
    return pl.pallas_call(
        _outproj_kernel,
        grid=(t // tm,),
        in_specs=[row(D_MODEL), row(SSM_D_INNER), row(ATTN_WIDTH),
                  pl.BlockSpec((D_MODEL, D_MODEL), lambda i: (0, 0),
                               pipeline_mode=pl.Buffered(1)),
                  pl.BlockSpec((1, D_MODEL), lambda i: (0, 0))],
        out_specs=[row(D_MODEL), row(D_MODEL)],
        out_shape=[jax.ShapeDtypeStruct((t, D_MODEL), F32),
                   jax.ShapeDtypeStruct((t, D_MODEL), BF16)],
        compiler_params=pltpu.CompilerParams(
            dimension_semantics=("parallel",), vmem_limit_bytes=VMEM_LIMIT),
        name="outproj",
    )(x2d, y_ssm, y_att, w, g)


def _mlp_kernel(x1_ref, h2_ref, wu_ref, wd_ref, g_ref, o_ref):
    j = pl.program_id(1)

    @pl.when(j == 0)
    def _():
        o_ref[...] = x1_ref[...]

    u = jnp.maximum(_dot(h2_ref[...], wu_ref[...]), 0.0)
    o_ref[...] += _dot((u * u).astype(BF16), wd_ref[...])

    @pl.when(j == pl.num_programs(1) - 1)
    def _():
        x2 = o_ref[...]
        o_ref[...] = x2 * _rms_scale(x2) * g_ref[...]


def _mlp(x1, h2, w_up, w_down, g):
    t = x1.shape[0]
    tm, tf = TM_MLP, TF_MLP
    return pl.pallas_call(
        _mlp_kernel,
        grid=(t // tm, D_FF // tf),
        in_specs=[pl.BlockSpec((tm, D_MODEL), lambda i, j: (i, 0)),
                  pl.BlockSpec((tm, D_MODEL), lambda i, j: (i, 0)),
                  pl.BlockSpec((D_MODEL, tf), lambda i, j: (0, j)),
                  pl.BlockSpec((tf, D_MODEL), lambda i, j: (j, 0)),
                  pl.BlockSpec((1, D_MODEL), lambda i, j: (0, 0))],
        out_specs=pl.BlockSpec((tm, D_MODEL), lambda i, j: (i, 0)),
        out_shape=jax.ShapeDtypeStruct((t, D_MODEL), F32),
        compiler_params=pltpu.CompilerParams(
            dimension_semantics=("parallel", "arbitrary"), vmem_limit_bytes=VMEM_LIMIT),
        name="mlp",
    )(x1, h2, w_up, w_down, g)


def _row(v, width=None):
    v = v.astype(F32).reshape(1, -1)
    if width is not None and v.shape[1] < width:
        v = jnp.pad(v, ((0, 0), (0, width - v.shape[1])))
    return v


def kernel(x, mix_norm_g, w_in, conv_w, conv_b, dt_bias, A_log, D_skip, ssm_norm_g,
           attn_sinks, attn_out_norm_g, w_out, mlp_norm_g, w_up, w_down, final_norm_g):
    b, l, d = x.shape
    assert d == D_MODEL and l % SSM_CHUNK == 0 and l % TM_PROJ == 0
    assert w_in.shape[0] == 1, "one layer"
    x2d = x.reshape(b * l, d)

    wi = w_in[0]
    o_dt = SSM_D_INNER + SSM_CONV_DIM
    o_q = o_dt + SSM_N_HEADS
    wa = wi[:, :o_dt].astype(BF16)
    wb = wi[:, o_q:].astype(BF16)
    wdt = jnp.pad(wi[:, o_dt:o_q], ((0, 0), (0, DT_PAD - SSM_N_HEADS))).astype(BF16)

    z, xbc, dt, y_att = _inproj_attn(x2d, _row(mix_norm_g[0]), wa, wb, wdt,
                                     attn_sinks[0].astype(F32), _row(attn_out_norm_g[0]), l)

    y_ssm, w_up16, w_down16, w_out16 = _ssd(
        xbc.reshape(b, l, -1), z.reshape(b, l, -1), dt.reshape(b, l, -1),
        conv_w[0].astype(F32), _row(conv_b[0]),
        _row(dt_bias[0], DT_PAD), _row(A_log[0], DT_PAD),
        _row(jnp.repeat(D_skip[0], SSM_HEAD_DIM)), _row(ssm_norm_g[0]),
        w_up[0], w_down[0], w_out[0])

    x1, h2 = _outproj(x2d, y_ssm.reshape(b * l, -1), y_att, w_out16, _row(mlp_norm_g[0]))
    out = _mlp(x1, h2, w_up16, w_down16, _row(final_norm_g))
    return out.reshape(b, l, d)
```

```python
import functools

import jax
import jax.numpy as jnp
from jax import lax
from jax.experimental import pallas as pl
from jax.experimental.pallas import tpu as pltpu

F32 = jnp.float32
BF16 = jnp.bfloat16

D_MODEL = 2048
SSM_D_INNER = 1024
SSM_HEAD_DIM = 64
SSM_N_HEADS = 16
SSM_N_GROUPS = 4
SSM_D_STATE = 128
SSM_CONV = 4
SSM_CHUNK = 128
SSM_GN = SSM_N_GROUPS * SSM_D_STATE
SSM_CONV_DIM = SSM_D_INNER + 2 * SSM_GN
SSM_GROUP_WIDTH = SSM_D_INNER // SSM_N_GROUPS
ATTN_WIDTH = 1024
ATTN_HEAD_DIM = 64
ATTN_N_HEADS = 16
ATTN_N_KV = 2
ATTN_KV_WIDTH = 2 * ATTN_N_KV * ATTN_HEAD_DIM
ATTN_BLOCK = 128
D_FF = 8192
EPS = 1e-5

LANES = 128
DT_PAD = LANES
SUBLANES = 8
MXU_WIDTH = 256
PROJ_ATTN_PATTERN = "abbbb"
VMEM_LIMIT = 56 * 1024 * 1024

SCAN_PROJ_PATTERN = "b" + 2 * "abaababaabaa" + "b"

TM_PROJ = 512
TM_SSD = 256
TM_MLP = 512
TF_MLP = 1024


def _dot(a, b):
    return jnp.dot(a, b, preferred_element_type=F32)


def _split3(x):
    hi = x.astype(BF16)
    r = x - hi.astype(F32)
    mid = r.astype(BF16)
    lo = (r - mid.astype(F32)).astype(BF16)
    return hi, mid, lo


def _rms_scale(x):
    return lax.rsqrt(jnp.mean(x * x, axis=-1, keepdims=True) + EPS)


def _silu(x):
    return x * (1.0 / (1.0 + jnp.exp(-x)))


def _interleave(gen_a, gen_b, pattern):
    gens = {"a": gen_a, "b": gen_b}
    live = {"a": True, "b": True}
    while any(live.values()):
        for key in pattern:
            if live[key]:
                live[key] = next(gens[key], _DONE) is not _DONE


_DONE = object()


def _attn_block_stages(q_blk, kv_prev, kv_cur, first_block, sink_ref, emit):
    QB = ATTN_BLOCK
    D = ATTN_HEAD_DIM
    kv = jnp.concatenate([kv_prev, kv_cur], axis=0).astype(F32)
    k01 = kv[:, :2 * D]
    v01 = kv[:, 2 * D:]
    k_t = k01.T
    v01r = pltpu.roll(v01, D, axis=1)

    lane_kv = lax.broadcasted_iota(jnp.int32, (2 * QB, LANES), 1)
    first_kv = lane_kv < D
    ones_e = first_kv.astype(BF16)
    ones_o = 1 - ones_e
    zero_t = jnp.zeros((D, 2 * QB), BF16)

    i = lax.broadcasted_iota(jnp.int32, (QB, 2 * QB), 0)
    j = lax.broadcasted_iota(jnp.int32, (QB, 2 * QB), 1)
    valid = (j > i) & (j <= i + QB) & ((j >= QB) | jnp.logical_not(first_block))
    lane_q = lax.broadcasted_iota(jnp.int32, (QB, LANES), 1)
    first_q = lane_q < D

    per_kv = ATTN_N_HEADS // ATTN_N_KV // 2
    rhs, vb = [], []
    for g in range(ATTN_N_KV):
        kg_t = k_t[g * D:(g + 1) * D, :].astype(BF16)
        rhs.append(jnp.concatenate([jnp.concatenate([kg_t, zero_t], axis=1),
                                    jnp.concatenate([zero_t, kg_t], axis=1)], axis=0))
        if g == 0:
            v_e = jnp.where(first_kv, v01, 0.0)
            v_o = jnp.where(first_kv, 0.0, v01r)
        else:
            v_e = jnp.where(first_kv, v01r, 0.0)
            v_o = jnp.where(first_kv, 0.0, v01)
        vb.append(jnp.concatenate(
            [jnp.concatenate([v_e.astype(BF16), ones_e], axis=1),
             jnp.concatenate([v_o.astype(BF16), ones_o], axis=1)], axis=0))
    yield

    def scores(jp):
        s = _dot(q_blk(jp), rhs[jp // per_kv]) * (D ** -0.5)
        s_e = jnp.where(valid, s[:, :2 * QB], -jnp.inf)
        s_o = jnp.where(valid, s[:, 2 * QB:], -jnp.inf)
        sink_e = sink_ref[2 * jp]
        sink_o = sink_ref[2 * jp + 1]
        m_e = jnp.maximum(jnp.max(s_e, axis=-1, keepdims=True), sink_e)
        m_o = jnp.maximum(jnp.max(s_o, axis=-1, keepdims=True), sink_o)
        p = jnp.concatenate([jnp.exp(s_e - m_e), jnp.exp(s_o - m_o)], axis=1).astype(BF16)
        sink_term = jnp.where(first_q, jnp.exp(sink_e - m_e), jnp.exp(sink_o - m_o))
        return p, sink_term

    def values(jp, p, sink_term):
        o = _dot(p, vb[jp // per_kv])
        return o[:, :LANES] / (o[:, LANES:] + sink_term)

    n_pairs = ATTN_N_KV * per_kv
    outs = []
    pending = scores(0)
    yield
    for jp in range(1, n_pairs):
        nxt = scores(jp)
        yield
        outs.append(values(jp - 1, *pending))
        pending = nxt
        yield
    outs.append(values(n_pairs - 1, *pending))
    emit(jnp.concatenate(outs, axis=1))
    yield


def _inproj_attn_kernel(sink_ref, x_ref, g_ref, wa_ref, wb_ref, wdt_ref, ng_ref, wo_ref,
                        z_ref, xbc_ref, dt_ref, yatt_ref, wo16_ref,
                        cur_scr, prev_scr, kvtail_scr, *, tiles_per_seq):
    s = pl.program_id(0)
    n_tiles = pl.num_programs(0) - 1
    blocks = TM_PROJ // ATTN_BLOCK
    tail = slice(TM_PROJ - ATTN_BLOCK, TM_PROJ)

    wo16_ref[...] = wo_ref[...].astype(BF16)

    def project():
        x = x_ref[...]
        h = (x * _rms_scale(x) * g_ref[...]).astype(BF16)
        yield
        for w_ref, dsts in ((wa_ref, ((z_ref, 0, SSM_D_INNER), (xbc_ref, SSM_D_INNER, SSM_CONV_DIM))),
                            (wb_ref, ((cur_scr, 0, ATTN_WIDTH + ATTN_KV_WIDTH),))):
            for dst, base, width in dsts:
                for lo in range(0, width, MXU_WIDTH):
                    dst[:, lo:lo + MXU_WIDTH] = _dot(
                        h, w_ref[:, base + lo:base + lo + MXU_WIDTH]).astype(BF16)
                    yield
        dt_ref[...] = _dot(h, wdt_ref[...])
        yield

    def rotate_buffers():
        kvtail_scr[...] = prev_scr[tail, ATTN_WIDTH:]
        prev_scr[...] = cur_scr[...]

    def attend():
        seq_start = ((s - 1) % tiles_per_seq) == 0
        for r in range(blocks):
            lo = r * ATTN_BLOCK
            kv_cur = prev_scr[lo:lo + ATTN_BLOCK, ATTN_WIDTH:]
            if r == 0:
                kv_prev = kvtail_scr[...]
                first_block = seq_start
            else:
                kv_prev = prev_scr[lo - ATTN_BLOCK:lo, ATTN_WIDTH:]
                first_block = False
            q_blk = lambda jp, lo=lo: prev_scr[lo:lo + ATTN_BLOCK, jp * LANES:(jp + 1) * LANES]

            def emit(y, lo=lo):
                yatt_ref[lo:lo + ATTN_BLOCK, :] = (y * _rms_scale(y) * ng_ref[...]).astype(BF16)

            yield from _attn_block_stages(q_blk, kv_prev, kv_cur, first_block, sink_ref, emit)

    @pl.when(s == 0)
    def _():
        prev_scr[tail, ATTN_WIDTH:] = jnp.zeros((ATTN_BLOCK, ATTN_KV_WIDTH), BF16)
        for _ in project():
            pass

    @pl.when((s > 0) & (s < n_tiles))
    def _():
        rotate_buffers()
        _interleave(project(), attend(), PROJ_ATTN_PATTERN)

    @pl.when(s == n_tiles)
    def _():
        rotate_buffers()
        for _ in attend():
            pass


def _inproj_attn(x2d, g, wa, wb, wdt, sinks, attn_g, w_out, seq_len):
    t = x2d.shape[0]
    tm = TM_PROJ
    n_tiles = t // tm
    cur = lambda width: pl.BlockSpec((tm, width), lambda i: (jnp.minimum(i, n_tiles - 1), 0))
    prev = lambda width: pl.BlockSpec((tm, width), lambda i: (jnp.maximum(i - 1, 0), 0))
    resident = lambda width: pl.BlockSpec((D_MODEL, width), lambda i: (0, 0),
                                          pipeline_mode=pl.Buffered(1))
    vec = lambda width: pl.BlockSpec((1, width), lambda i: (0, 0))
    wo_rows = pl.BlockSpec((D_MODEL // n_tiles, D_MODEL),
                           lambda i: (jnp.minimum(i, n_tiles - 1), 0))
    return pl.pallas_call(
        functools.partial(_inproj_attn_kernel, tiles_per_seq=seq_len // tm),
        grid=(n_tiles + 1,),
        in_specs=[
            pl.BlockSpec(memory_space=pltpu.SMEM),
            cur(D_MODEL),
            vec(D_MODEL),
            resident(SSM_D_INNER + SSM_CONV_DIM),
            resident(ATTN_WIDTH + ATTN_KV_WIDTH),
            resident(DT_PAD),
            vec(ATTN_WIDTH),
            wo_rows,
        ],
        out_specs=[cur(SSM_D_INNER), cur(SSM_CONV_DIM), cur(DT_PAD), prev(ATTN_WIDTH), wo_rows],
        out_shape=[
            jax.ShapeDtypeStruct((t, SSM_D_INNER), BF16),
            jax.ShapeDtypeStruct((t, SSM_CONV_DIM), BF16),
            jax.ShapeDtypeStruct((t, DT_PAD), F32),
            jax.ShapeDtypeStruct((t, ATTN_WIDTH), BF16),
            jax.ShapeDtypeStruct((D_MODEL, D_MODEL), BF16),
        ],
        scratch_shapes=[pltpu.VMEM((tm, ATTN_WIDTH + ATTN_KV_WIDTH), BF16),
                        pltpu.VMEM((tm, ATTN_WIDTH + ATTN_KV_WIDTH), BF16),
                        pltpu.VMEM((ATTN_BLOCK, ATTN_KV_WIDTH), BF16)],
        compiler_params=pltpu.CompilerParams(
            dimension_semantics=("arbitrary",), vmem_limit_bytes=VMEM_LIMIT),
        name="inproj_attn",
    )(sinks, x2d, g, wa, wb, wdt, attn_g, w_out)


def _ssd_chunk_stages(xbc, z, dt_raw, cw_ref, cb_ref, dtb_ref, alog_ref, dskip_ref, ng_ref,
                      cbuf, state, emit):
    Q = SSM_CHUNK
    heads_per_group = SSM_N_HEADS // SSM_N_GROUPS
    dtr = dt_raw + dtb_ref[...]
    dt = jnp.maximum(dtr, 0.0) + jnp.log1p(jnp.exp(-jnp.abs(dtr)))
    a = dt * (-jnp.exp(alog_ref[...]))

    row = lax.broadcasted_iota(jnp.int32, (Q, Q), 0)
    col = lax.broadcasted_iota(jnp.int32, (Q, Q), 1)
    causal = row >= col
    tri = causal.astype(BF16)
    a_hi, a_mid, a_lo = _split3(a)
    a_cs = _dot(tri, a_hi) + _dot(tri, a_mid) + _dot(tri, a_lo)
    a_cs_t = a_cs.T
    yield

    cur = xbc.astype(F32)
    cbuf[SUBLANES:SUBLANES + Q, :] = cur
    acc = cb_ref[...] + cw_ref[SSM_CONV - 1:SSM_CONV, :] * cur
    for k in range(SSM_CONV - 1):
        lo = SUBLANES - (SSM_CONV - 1) + k
        acc = acc + cw_ref[k:k + 1, :] * cbuf[lo:lo + Q, :]
    cbuf[0:SUBLANES, :] = cur[Q - SUBLANES:Q, :]
    xc = _silu(acc)
    xs = xc[:, :SSM_D_INNER]
    yield

    er = lax.broadcasted_iota(jnp.int32, (LANES, SSM_D_INNER), 0)
    ec = lax.broadcasted_iota(jnp.int32, (LANES, SSM_D_INNER), 1)
    expand = ((ec // SSM_HEAD_DIM) == er).astype(BF16)
    stacked = jnp.concatenate([dt, a_cs], axis=0)
    s_hi, s_mid, s_lo = _split3(stacked)
    ex = _dot(s_hi, expand) + _dot(s_mid, expand) + _dot(s_lo, expand)
    dt_x = ex[:Q]
    acs_x = ex[Q:]
    acs_last = acs_x[Q - 1:Q, :]
    yield

    xd = xs * dt_x
    xd16 = xd.astype(BF16)
    xdd16 = (xd * jnp.exp(acs_last - acs_x)).astype(BF16)
    exp_acs = jnp.exp(acs_x)
    chunk_decay = jnp.exp(acs_last)

    lane = lax.broadcasted_iota(jnp.int32, (Q, LANES), 1)
    first_half = lane < SSM_HEAD_DIM
    zero16 = jnp.zeros((Q, LANES), BF16)

    b16, c16, cb = [], [], []
    for g in range(SSM_N_GROUPS):
        b16.append(xc[:, SSM_D_INNER + g * SSM_D_STATE:
                      SSM_D_INNER + (g + 1) * SSM_D_STATE].astype(BF16))
        c16.append(xc[:, SSM_D_INNER + SSM_GN + g * SSM_D_STATE:
                      SSM_D_INNER + SSM_GN + (g + 1) * SSM_D_STATE].astype(BF16))
        cb.append(lax.dot_general(c16[g], b16[g], (((1,), (1,)), ((), ())),
                                  preferred_element_type=F32))
    yield

    y_off = []
    for g in range(SSM_N_GROUPS):
        gl = g * SSM_GROUP_WIDTH
        st = state[:, gl:gl + SSM_GROUP_WIDTH]
        y_off.append(_dot(c16[g], st.astype(BF16)) * exp_acs[:, gl:gl + SSM_GROUP_WIDTH])
        upd = lax.dot_general(b16[g], xdd16[:, gl:gl + SSM_GROUP_WIDTH],
                              (((0,), (0,)), ((), ())), preferred_element_type=F32)
        state[:, gl:gl + SSM_GROUP_WIDTH] = chunk_decay[:, gl:gl + SSM_GROUP_WIDTH] * st + upd
    yield

    y_parts = []
    for g in range(SSM_N_GROUPS):
        ms = []
        for r in range(heads_per_group):
            h = g * heads_per_group + r
            seg = a_cs[:, h:h + 1] - a_cs_t[h:h + 1, :]
            lmat = jnp.exp(jnp.where(causal, seg, -jnp.inf))
            ms.append((cb[g] * lmat).astype(BF16))
        gl = g * SSM_GROUP_WIDTH
        yd = []
        for pr in range(2):
            xp = xd16[:, gl + pr * LANES:gl + (pr + 1) * LANES]
            rhs = jnp.concatenate([jnp.where(first_half, xp, zero16),
                                   jnp.where(first_half, zero16, xp)], axis=0)
            lhs = jnp.concatenate([ms[2 * pr], ms[2 * pr + 1]], axis=1)
            yd.append(_dot(lhs, rhs))
        y_parts.append(jnp.concatenate(yd, axis=1) + y_off[g])
        if g % 2 == 1:
            yield

    y = jnp.concatenate(y_parts, axis=1) + xs * dskip_ref[...]
    y = y * _silu(z.astype(F32))
    outs = []
    for g in range(SSM_N_GROUPS):
        yg = y[:, g * SSM_GROUP_WIDTH:(g + 1) * SSM_GROUP_WIDTH]
        outs.append(yg * _rms_scale(yg))
    emit(jnp.concatenate(outs, axis=1) * ng_ref[...])
    yield


def _ssd_outproj_kernel(xbc_ref, z_ref, dt_ref, cw_ref, cb_ref, dtb_ref, alog_ref, dskip_ref,
                        ng_ref, x_ref, ya_ref, wo_ref, g2_ref, wu_ref, wd_ref,
                        x1_ref, h2_ref, wu16_ref, wd16_ref,
                        cbuf, state, ys_cur, ys_prev, *, tiles_per_seq):
    s = pl.program_id(0)
    n_tiles = pl.num_programs(0) - 1
    Q = SSM_CHUNK

    wu16_ref[...] = wu_ref[...].astype(BF16)
    wd16_ref[...] = wd_ref[...].astype(BF16)

    @pl.when((s % tiles_per_seq) == 0)
    def _():
        cbuf[0:SUBLANES, :] = jnp.zeros((SUBLANES, SSM_CONV_DIM), F32)
        state[...] = jnp.zeros_like(state)

    def scan():
        for c in range(TM_SSD // Q):
            lo = c * Q

            def emit(y, lo=lo):
                ys_cur[lo:lo + Q, :] = y.astype(BF16)

            yield from _ssd_chunk_stages(
                xbc_ref[lo:lo + Q, :], z_ref[lo:lo + Q, :], dt_ref[lo:lo + Q, :],
                cw_ref, cb_ref, dtb_ref, alog_ref, dskip_ref, ng_ref, cbuf, state, emit)

    def project():
        y = jnp.concatenate([ys_prev[...], ya_ref[...]], axis=1)
        yield
        for lo in range(0, D_MODEL, MXU_WIDTH):
            x1_ref[:, lo:lo + MXU_WIDTH] = (x_ref[:, lo:lo + MXU_WIDTH]
                                            + _dot(y, wo_ref[:, lo:lo + MXU_WIDTH]))
            yield
        x1 = x1_ref[...]
        h2_ref[...] = (x1 * _rms_scale(x1) * g2_ref[...]).astype(BF16)
        yield

    @pl.when(s == 0)
    def _():
        for _ in scan():
            pass

    @pl.when((s > 0) & (s < n_tiles))
    def _():
        ys_prev[...] = ys_cur[...]
        _interleave(scan(), project(), SCAN_PROJ_PATTERN)

    @pl.when(s == n_tiles)
    def _():
        ys_prev[...] = ys_cur[...]
        for _ in project():
            pass


def _ssd_outproj(x2d, xbc, z, dt, y_att, conv_w, conv_b, dt_bias, a_log, d_skip, norm_g,
                 w_out16, mlp_g, w_up, w_down, seq_len):
    t = x2d.shape[0]
    tm = TM_SSD
    n_tiles = t // tm
    cur = lambda width: pl.BlockSpec((tm, width), lambda i: (jnp.minimum(i, n_tiles - 1), 0))
    prev = lambda width: pl.BlockSpec((tm, width), lambda i: (jnp.maximum(i - 1, 0), 0))
    full = lambda r, width: pl.BlockSpec((r, width), lambda i: (0, 0))
    part = lambda r, width: pl.BlockSpec((r, width), lambda i: (jnp.minimum(i, n_tiles - 1), 0))
    ru, rd = D_MODEL // n_tiles, D_FF // n_tiles
    return pl.pallas_call(
        functools.partial(_ssd_outproj_kernel, tiles_per_seq=seq_len // tm),
        grid=(n_tiles + 1,),
        in_specs=[cur(SSM_CONV_DIM), cur(SSM_D_INNER), cur(DT_PAD),
                  full(SSM_CONV, SSM_CONV_DIM), full(1, SSM_CONV_DIM),
                  full(1, DT_PAD), full(1, DT_PAD),
                  full(1, SSM_D_INNER), full(1, SSM_D_INNER),
                  prev(D_MODEL), prev(ATTN_WIDTH),
                  pl.BlockSpec((D_MODEL, D_MODEL), lambda i: (0, 0), pipeline_mode=pl.Buffered(1)),
                  full(1, D_MODEL),
                  part(ru, D_FF), part(rd, D_MODEL)],
        out_specs=[prev(D_MODEL), prev(D_MODEL), part(ru, D_FF), part(rd, D_MODEL)],
        out_shape=[jax.ShapeDtypeStruct((t, D_MODEL), F32),
                   jax.ShapeDtypeStruct((t, D_MODEL), BF16),
                   jax.ShapeDtypeStruct((D_MODEL, D_FF), BF16),
                   jax.ShapeDtypeStruct((D_FF, D_MODEL), BF16)],
        scratch_shapes=[pltpu.VMEM((SUBLANES + SSM_CHUNK, SSM_CONV_DIM), F32),
                        pltpu.VMEM((SSM_D_STATE, SSM_D_INNER), F32),
                        pltpu.VMEM((tm, SSM_D_INNER), BF16),
                        pltpu.VMEM((tm, SSM_D_INNER), BF16)],
        compiler_params=pltpu.CompilerParams(
            dimension_semantics=("arbitrary",), vmem_limit_bytes=VMEM_LIMIT),
        name="ssd_outproj",
    )(xbc, z, dt, conv_w, conv_b, dt_bias, a_log, d_skip, norm_g,
      x2d, y_att, w_out16, mlp_g, w_up, w_down)


def _mlp_kernel(x1_ref, h2_ref, wu_ref, wd_ref, g_ref, o_ref):
    j = pl.program_id(1)

    @pl.when(j == 0)
    def _():
        o_ref[...] = x1_ref[...]

    u = jnp.maximum(_dot(h2_ref[...], wu_ref[...]), 0.0)
    o_ref[...] += _dot((u * u).astype(BF16), wd_ref[...])

    @pl.when(j == pl.num_programs(1) - 1)
    def _():
        x2 = o_ref[...]
        o_ref[...] = x2 * _rms_scale(x2) * g_ref[...]


def _mlp(x1, h2, w_up, w_down, g):
    t = x1.shape[0]
    tm, tf = TM_MLP, TF_MLP
    return pl.pallas_call(
        _mlp_kernel,
        grid=(t // tm, D_FF // tf),
        in_specs=[pl.BlockSpec((tm, D_MODEL), lambda i, j: (i, 0)),
                  pl.BlockSpec((tm, D_MODEL), lambda i, j: (i, 0)),
                  pl.BlockSpec((D_MODEL, tf), lambda i, j: (0, j)),
                  pl.BlockSpec((tf, D_MODEL), lambda i, j: (j, 0)),
                  pl.BlockSpec((1, D_MODEL), lambda i, j: (0, 0))],
        out_specs=pl.BlockSpec((tm, D_MODEL), lambda i, j: (i, 0)),
        out_shape=jax.ShapeDtypeStruct((t, D_MODEL), F32),
        compiler_params=pltpu.CompilerParams(
            dimension_semantics=("parallel", "arbitrary"), vmem_limit_bytes=VMEM_LIMIT),
        name="mlp",
    )(x1, h2, w_up, w_down, g)


def _row(v, width=None):
    v = v.astype(F32).reshape(1, -1)
    if width is not None and v.shape[1] < width:
        v = jnp.pad(v, ((0, 0), (0, width - v.shape[1])))
    return v


def kernel(x, mix_norm_g, w_in, conv_w, conv_b, dt_bias, A_log, D_skip, ssm_norm_g,
           attn_sinks, attn_out_norm_g, w_out, mlp_norm_g, w_up, w_down, final_norm_g):
    b, l, d = x.shape
    assert d == D_MODEL and l % SSM_CHUNK == 0 and l % TM_PROJ == 0
    assert w_in.shape[0] == 1, "one layer"
    x2d = x.reshape(b * l, d)

    wi = w_in[0]
    o_dt = SSM_D_INNER + SSM_CONV_DIM
    o_q = o_dt + SSM_N_HEADS
    wa = wi[:, :o_dt].astype(BF16)
    wb = wi[:, o_q:].astype(BF16)
    wdt = jnp.pad(wi[:, o_dt:o_q], ((0, 0), (0, DT_PAD - SSM_N_HEADS))).astype(BF16)

    z, xbc, dt, y_att, w_out16 = _inproj_attn(
        x2d, _row(mix_norm_g[0]), wa, wb, wdt,
        attn_sinks[0].astype(F32), _row(attn_out_norm_g[0]), w_out[0], l)

    x1, h2, w_up16, w_down16 = _ssd_outproj(
        x2d, xbc, z, dt, y_att,
        conv_w[0].astype(F32), _row(conv_b[0]),
        _row(dt_bias[0], DT_PAD), _row(A_log[0], DT_PAD),
        _row(jnp.repeat(D_skip[0], SSM_HEAD_DIM)), _row(ssm_norm_g[0]),
        w_out16, _row(mlp_norm_g[0]), w_up[0], w_down[0], l)
    out = _mlp(x1, h2, w_up16, w_down16, _row(final_norm_g))
    return out.reshape(b, l, d)
```

```python
import functools

import jax
import jax.numpy as jnp
from jax import lax
from jax.experimental import pallas as pl
from jax.experimental.pallas import tpu as pltpu

F32 = jnp.float32
BF16 = jnp.bfloat16

D_MODEL = 2048
SSM_D_INNER = 1024
SSM_HEAD_DIM = 64
SSM_N_HEADS = 16
SSM_N_GROUPS = 4
SSM_D_STATE = 128
SSM_CONV = 4
SSM_CHUNK = 128
SSM_GN = SSM_N_GROUPS * SSM_D_STATE
SSM_CONV_DIM = SSM_D_INNER + 2 * SSM_GN
SSM_GROUP_WIDTH = SSM_D_INNER // SSM_N_GROUPS
ATTN_WIDTH = 1024
ATTN_HEAD_DIM = 64
ATTN_N_HEADS = 16
ATTN_N_KV = 2
ATTN_KV_WIDTH = 2 * ATTN_N_KV * ATTN_HEAD_DIM
ATTN_BLOCK = 128
D_FF = 8192
EPS = 1e-5
LOG2E = 1.4426950408889634

LANES = 128
DT_PAD = LANES
SUBLANES = 8
MXU_WIDTH = 256
PROJ_ATTN_PATTERN = "abbbb"
VMEM_LIMIT = 56 * 1024 * 1024

SCAN_PROJ_PATTERN = "b" + 2 * "abaababaabaa" + "b"

TM_PROJ = 512
TM_SSD = 256
TM_MLP = 512
TF_MLP = 1024


def _dot(a, b):
    return jnp.dot(a, b, preferred_element_type=F32)


def _split3(x):
    hi = x.astype(BF16)
    r = x - hi.astype(F32)
    mid = r.astype(BF16)
    lo = (r - mid.astype(F32)).astype(BF16)
    return hi, mid, lo


def _rms_scale(x):
    return lax.rsqrt(jnp.mean(x * x, axis=-1, keepdims=True) + EPS)


def _silu(x):
    hx = 0.5 * x
    return hx + hx * jnp.tanh(hx)


def _interleave(gen_a, gen_b, pattern):
    gens = {"a": gen_a, "b": gen_b}
    live = {"a": True, "b": True}
    while any(live.values()):
        for key in pattern:
            if live[key]:
                live[key] = next(gens[key], _DONE) is not _DONE


_DONE = object()


def _attn_block_stages(q_blk, kv_prev, kv_cur, first_block, sink_ref, emit):
    QB = ATTN_BLOCK
    D = ATTN_HEAD_DIM
    kv = jnp.concatenate([kv_prev, kv_cur], axis=0).astype(F32)
    k01 = kv[:, :2 * D]
    v01 = kv[:, 2 * D:]
    k_t = k01.T
    v01r = pltpu.roll(v01, D, axis=1)

    lane_kv = lax.broadcasted_iota(jnp.int32, (2 * QB, LANES), 1)
    first_kv = lane_kv < D
    ones_e = first_kv.astype(BF16)
    ones_o = 1 - ones_e
    zero_t = jnp.zeros((D, 2 * QB), BF16)

    i = lax.broadcasted_iota(jnp.int32, (QB, 2 * QB), 0)
    j = lax.broadcasted_iota(jnp.int32, (QB, 2 * QB), 1)
    valid = (j > i) & (j <= i + QB) & ((j >= QB) | jnp.logical_not(first_block))
    lane_q = lax.broadcasted_iota(jnp.int32, (QB, LANES), 1)
    first_q = lane_q < D

    per_kv = ATTN_N_HEADS // ATTN_N_KV // 2
    rhs, vb = [], []
    for g in range(ATTN_N_KV):
        kg_t = k_t[g * D:(g + 1) * D, :].astype(BF16)
        rhs.append(jnp.concatenate([jnp.concatenate([kg_t, zero_t], axis=1),
                                    jnp.concatenate([zero_t, kg_t], axis=1)], axis=0))
        if g == 0:
            v_e = jnp.where(first_kv, v01, 0.0)
            v_o = jnp.where(first_kv, 0.0, v01r)
        else:
            v_e = jnp.where(first_kv, v01r, 0.0)
            v_o = jnp.where(first_kv, 0.0, v01)
        vb.append(jnp.concatenate(
            [jnp.concatenate([v_e.astype(BF16), ones_e], axis=1),
             jnp.concatenate([v_o.astype(BF16), ones_o], axis=1)], axis=0))
    yield

    def scores(jp):
        s = _dot(q_blk(jp), rhs[jp // per_kv]) * (D ** -0.5)
        s_e = jnp.where(valid, s[:, :2 * QB], -jnp.inf)
        s_o = jnp.where(valid, s[:, 2 * QB:], -jnp.inf)
        sink_e = sink_ref[2 * jp]
        sink_o = sink_ref[2 * jp + 1]
        m_e = jnp.maximum(jnp.max(s_e, axis=-1, keepdims=True), sink_e)
        m_o = jnp.maximum(jnp.max(s_o, axis=-1, keepdims=True), sink_o)
        p = jnp.concatenate([jnp.exp(s_e - m_e), jnp.exp(s_o - m_o)], axis=1).astype(BF16)
        sink_term = jnp.where(first_q, jnp.exp(sink_e - m_e), jnp.exp(sink_o - m_o))
        return p, sink_term

    def values(jp, p, sink_term):
        o = _dot(p, vb[jp // per_kv])
        return o[:, :LANES] / (o[:, LANES:] + sink_term)

    n_pairs = ATTN_N_KV * per_kv
    outs = []
    pending = scores(0)
    yield
    for jp in range(1, n_pairs):
        nxt = scores(jp)
        yield
        outs.append(values(jp - 1, *pending))
        pending = nxt
        yield
    outs.append(values(n_pairs - 1, *pending))
    emit(jnp.concatenate(outs, axis=1))
    yield


def _inproj_attn_kernel(sink_ref, x_ref, g_ref, wa_ref, wb_ref, wdt_ref, ng_ref, wo_ref,
                        z_ref, xbc_ref, dt_ref, yatt_ref, wo16_ref,
                        cur_scr, prev_scr, kvtail_scr, *, tiles_per_seq):
    s = pl.program_id(0)
    n_tiles = pl.num_programs(0) - 1
    blocks = TM_PROJ // ATTN_BLOCK
    tail = slice(TM_PROJ - ATTN_BLOCK, TM_PROJ)

    wo16_ref[...] = wo_ref[...].astype(BF16)

    def project():
        x = x_ref[...]
        h = (x * _rms_scale(x) * g_ref[...]).astype(BF16)
        yield
        for w_ref, dsts in ((wa_ref, ((z_ref, 0, SSM_D_INNER), (xbc_ref, SSM_D_INNER, SSM_CONV_DIM))),
                            (wb_ref, ((cur_scr, 0, ATTN_WIDTH + ATTN_KV_WIDTH),))):
            for dst, base, width in dsts:
                for lo in range(0, width, MXU_WIDTH):
                    dst[:, lo:lo + MXU_WIDTH] = _dot(
                        h, w_ref[:, base + lo:base + lo + MXU_WIDTH]).astype(BF16)
                    yield
        dt_ref[...] = _dot(h, wdt_ref[...])
        yield

    def rotate_buffers():
        kvtail_scr[...] = prev_scr[tail, ATTN_WIDTH:]
        prev_scr[...] = cur_scr[...]

    def attend():
        seq_start = ((s - 1) % tiles_per_seq) == 0
        for r in range(blocks):
            lo = r * ATTN_BLOCK
            kv_cur = prev_scr[lo:lo + ATTN_BLOCK, ATTN_WIDTH:]
            if r == 0:
                kv_prev = kvtail_scr[...]
                first_block = seq_start
            else:
                kv_prev = prev_scr[lo - ATTN_BLOCK:lo, ATTN_WIDTH:]
                first_block = False
            q_blk = lambda jp, lo=lo: prev_scr[lo:lo + ATTN_BLOCK, jp * LANES:(jp + 1) * LANES]

            def emit(y, lo=lo):
                yatt_ref[lo:lo + ATTN_BLOCK, :] = (y * _rms_scale(y) * ng_ref[...]).astype(BF16)

            yield from _attn_block_stages(q_blk, kv_prev, kv_cur, first_block, sink_ref, emit)

    @pl.when(s == 0)
    def _():
        prev_scr[tail, ATTN_WIDTH:] = jnp.zeros((ATTN_BLOCK, ATTN_KV_WIDTH), BF16)
        for _ in project():
            pass

    @pl.when((s > 0) & (s < n_tiles))
    def _():
        rotate_buffers()
        _interleave(project(), attend(), PROJ_ATTN_PATTERN)

    @pl.when(s == n_tiles)
    def _():
        rotate_buffers()
        for _ in attend():
            pass


def _inproj_attn(x2d, g, wa, wb, wdt, sinks, attn_g, w_out, seq_len):
    t = x2d.shape[0]
    tm = TM_PROJ
    n_tiles = t // tm
    cur = lambda width: pl.BlockSpec((tm, width), lambda i: (jnp.minimum(i, n_tiles - 1), 0))
    prev = lambda width: pl.BlockSpec((tm, width), lambda i: (jnp.maximum(i - 1, 0), 0))
    resident = lambda width: pl.BlockSpec((D_MODEL, width), lambda i: (0, 0),
                                          pipeline_mode=pl.Buffered(1))
    vec = lambda width: pl.BlockSpec((1, width), lambda i: (0, 0))
    wo_rows = pl.BlockSpec((D_MODEL // n_tiles, D_MODEL),
                           lambda i: (jnp.minimum(i, n_tiles - 1), 0))
    return pl.pallas_call(
        functools.partial(_inproj_attn_kernel, tiles_per_seq=seq_len // tm),
        grid=(n_tiles + 1,),
        in_specs=[
            pl.BlockSpec(memory_space=pltpu.SMEM),
            cur(D_MODEL),
            vec(D_MODEL),
            resident(SSM_D_INNER + SSM_CONV_DIM),
            resident(ATTN_WIDTH + ATTN_KV_WIDTH),
            resident(DT_PAD),
            vec(ATTN_WIDTH),
            wo_rows,
        ],
        out_specs=[cur(SSM_D_INNER), cur(SSM_CONV_DIM), cur(DT_PAD), prev(ATTN_WIDTH), wo_rows],
        out_shape=[
            jax.ShapeDtypeStruct((t, SSM_D_INNER), BF16),
            jax.ShapeDtypeStruct((t, SSM_CONV_DIM), BF16),
            jax.ShapeDtypeStruct((t, DT_PAD), F32),
            jax.ShapeDtypeStruct((t, ATTN_WIDTH), BF16),
            jax.ShapeDtypeStruct((D_MODEL, D_MODEL), BF16),
        ],
        scratch_shapes=[pltpu.VMEM((tm, ATTN_WIDTH + ATTN_KV_WIDTH), BF16),
                        pltpu.VMEM((tm, ATTN_WIDTH + ATTN_KV_WIDTH), BF16),
                        pltpu.VMEM((ATTN_BLOCK, ATTN_KV_WIDTH), BF16)],
        compiler_params=pltpu.CompilerParams(
            dimension_semantics=("arbitrary",), vmem_limit_bytes=VMEM_LIMIT),
        name="inproj_attn",
    )(sinks, x2d, g, wa, wb, wdt, attn_g, w_out)


def _ssd_chunk_stages(xbc, z, dt_raw, cw_ref, cb_ref, dtb_ref, alog_ref, dskip_ref, ng_ref,
                      cbuf, state, emit):
    Q = SSM_CHUNK
    heads_per_group = SSM_N_HEADS // SSM_N_GROUPS
    dtr = dt_raw + dtb_ref[...]
    dt = jnp.maximum(dtr, 0.0) + jnp.log1p(jnp.exp(-jnp.abs(dtr)))
    a = dt * (-LOG2E * jnp.exp(alog_ref[...]))

    row = lax.broadcasted_iota(jnp.int32, (Q, Q), 0)
    col = lax.broadcasted_iota(jnp.int32, (Q, Q), 1)
    causal = row >= col
    tri = causal.astype(BF16)
    a_hi, a_mid, a_lo = _split3(a)
    a_cs = _dot(tri, a_hi) + _dot(tri, a_mid) + _dot(tri, a_lo)
    a_cs_t = a_cs.T
    yield

    cur = xbc.astype(F32)
    cbuf[SUBLANES:SUBLANES + Q, :] = cur
    acc = cb_ref[...] + cw_ref[SSM_CONV - 1:SSM_CONV, :] * cur
    for k in range(SSM_CONV - 1):
        lo = SUBLANES - (SSM_CONV - 1) + k
        acc = acc + cw_ref[k:k + 1, :] * cbuf[lo:lo + Q, :]
    cbuf[0:SUBLANES, :] = cur[Q - SUBLANES:Q, :]
    xc = _silu(acc)
    xs = xc[:, :SSM_D_INNER]
    yield

    er = lax.broadcasted_iota(jnp.int32, (LANES, SSM_D_INNER), 0)
    ec = lax.broadcasted_iota(jnp.int32, (LANES, SSM_D_INNER), 1)
    expand = ((ec // SSM_HEAD_DIM) == er).astype(BF16)
    stacked = jnp.concatenate([dt, a_cs], axis=0)
    s_hi, s_mid, _ = _split3(stacked)
    ex = _dot(s_hi, expand) + _dot(s_mid, expand)
    dt_x = ex[:Q]
    acs_x = ex[Q:]
    acs_last = acs_x[Q - 1:Q, :]
    yield

    xd = xs * dt_x
    xd16 = xd.astype(BF16)
    xdd16 = (xd * jnp.exp2(acs_last - acs_x)).astype(BF16)
    exp_acs = jnp.exp2(acs_x)
    chunk_decay = jnp.exp2(acs_last)

    lane = lax.broadcasted_iota(jnp.int32, (Q, LANES), 1)
    first_half = lane < SSM_HEAD_DIM
    zero16 = jnp.zeros((Q, LANES), BF16)

    b16, c16, cb = [], [], []
    for g in range(SSM_N_GROUPS):
        b16.append(xc[:, SSM_D_INNER + g * SSM_D_STATE:
                      SSM_D_INNER + (g + 1) * SSM_D_STATE].astype(BF16))
        c16.append(xc[:, SSM_D_INNER + SSM_GN + g * SSM_D_STATE:
                      SSM_D_INNER + SSM_GN + (g + 1) * SSM_D_STATE].astype(BF16))
        cb.append(lax.dot_general(c16[g], b16[g], (((1,), (1,)), ((), ())),
                                  preferred_element_type=F32))
    yield

    y_off = []
    for g in range(SSM_N_GROUPS):
        gl = g * SSM_GROUP_WIDTH
        st = state[:, gl:gl + SSM_GROUP_WIDTH]
        y_off.append(_dot(c16[g], st.astype(BF16)) * exp_acs[:, gl:gl + SSM_GROUP_WIDTH])
        upd = lax.dot_general(b16[g], xdd16[:, gl:gl + SSM_GROUP_WIDTH],
                              (((0,), (0,)), ((), ())), preferred_element_type=F32)
        state[:, gl:gl + SSM_GROUP_WIDTH] = chunk_decay[:, gl:gl + SSM_GROUP_WIDTH] * st + upd
    yield

    y_parts = []
    for g in range(SSM_N_GROUPS):
        ms = []
        for r in range(heads_per_group):
            h = g * heads_per_group + r
            seg = a_cs[:, h:h + 1] - a_cs_t[h:h + 1, :]
            lmat = jnp.exp2(jnp.where(causal, seg, -jnp.inf))
            ms.append((cb[g] * lmat).astype(BF16))
        gl = g * SSM_GROUP_WIDTH
        yd = []
        for pr in range(2):
            xp = xd16[:, gl + pr * LANES:gl + (pr + 1) * LANES]
            rhs = jnp.concatenate([jnp.where(first_half, xp, zero16),
                                   jnp.where(first_half, zero16, xp)], axis=0)
            lhs = jnp.concatenate([ms[2 * pr], ms[2 * pr + 1]], axis=1)
            yd.append(_dot(lhs, rhs))
        y_parts.append(jnp.concatenate(yd, axis=1) + y_off[g])
        if g % 2 == 1:
            yield

    y = jnp.concatenate(y_parts, axis=1) + xs * dskip_ref[...]
    y = y * _silu(z.astype(F32))
    outs = []
    for g in range(SSM_N_GROUPS):
        yg = y[:, g * SSM_GROUP_WIDTH:(g + 1) * SSM_GROUP_WIDTH]
        outs.append(yg * _rms_scale(yg))
    emit(jnp.concatenate(outs, axis=1) * ng_ref[...])
    yield


def _ssd_outproj_kernel(xbc_ref, z_ref, dt_ref, cw_ref, cb_ref, dtb_ref, alog_ref, dskip_ref,
                        ng_ref, x_ref, ya_ref, wo_ref, g2_ref, wu_ref, wd_ref,
                        x1_ref, h2_ref, wu16_ref, wd16_ref,
                        cbuf, state, ys_cur, ys_prev, *, tiles_per_seq):
    s = pl.program_id(0)
    n_tiles = pl.num_programs(0) - 1
    Q = SSM_CHUNK

    wu16_ref[...] = wu_ref[...].astype(BF16)
    wd16_ref[...] = wd_ref[...].astype(BF16)

    @pl.when((s % tiles_per_seq) == 0)
    def _():
        cbuf[0:SUBLANES, :] = jnp.zeros((SUBLANES, SSM_CONV_DIM), F32)
        state[...] = jnp.zeros_like(state)

    def scan():
        for c in range(TM_SSD // Q):
            lo = c * Q

            def emit(y, lo=lo):
                ys_cur[lo:lo + Q, :] = y.astype(BF16)

            yield from _ssd_chunk_stages(
                xbc_ref[lo:lo + Q, :], z_ref[lo:lo + Q, :], dt_ref[lo:lo + Q, :],
                cw_ref, cb_ref, dtb_ref, alog_ref, dskip_ref, ng_ref, cbuf, state, emit)

    def project():
        y = jnp.concatenate([ys_prev[...], ya_ref[...]], axis=1)
        yield
        for lo in range(0, D_MODEL, MXU_WIDTH):
            x1_ref[:, lo:lo + MXU_WIDTH] = (x_ref[:, lo:lo + MXU_WIDTH]
                                            + _dot(y, wo_ref[:, lo:lo + MXU_WIDTH]))
            yield
        x1 = x1_ref[...]
        h2_ref[...] = (x1 * _rms_scale(x1) * g2_ref[...]).astype(BF16)
        yield

    @pl.when(s == 0)
    def _():
        for _ in scan():
            pass

    @pl.when((s > 0) & (s < n_tiles))
    def _():
        ys_prev[...] = ys_cur[...]
        _interleave(scan(), project(), SCAN_PROJ_PATTERN)

    @pl.when(s == n_tiles)
    def _():
        ys_prev[...] = ys_cur[...]
        for _ in project():
            pass


def _ssd_outproj(x2d, xbc, z, dt, y_att, conv_w, conv_b, dt_bias, a_log, d_skip, norm_g,
                 w_out16, mlp_g, w_up, w_down, seq_len):
    t = x2d.shape[0]
    tm = TM_SSD
    n_tiles = t // tm
    cur = lambda width: pl.BlockSpec((tm, width), lambda i: (jnp.minimum(i, n_tiles - 1), 0))
    prev = lambda width: pl.BlockSpec((tm, width), lambda i: (jnp.maximum(i - 1, 0), 0))
    full = lambda r, width: pl.BlockSpec((r, width), lambda i: (0, 0))
    part = lambda r, width: pl.BlockSpec((r, width), lambda i: (jnp.minimum(i, n_tiles - 1), 0))
    ru, rd = D_MODEL // n_tiles, D_FF // n_tiles
    return pl.pallas_call(
        functools.partial(_ssd_outproj_kernel, tiles_per_seq=seq_len // tm),
        grid=(n_tiles + 1,),
        in_specs=[cur(SSM_CONV_DIM), cur(SSM_D_INNER), cur(DT_PAD),
                  full(SSM_CONV, SSM_CONV_DIM), full(1, SSM_CONV_DIM),
                  full(1, DT_PAD), full(1, DT_PAD),
                  full(1, SSM_D_INNER), full(1, SSM_D_INNER),
                  prev(D_MODEL), prev(ATTN_WIDTH),
                  pl.BlockSpec((D_MODEL, D_MODEL), lambda i: (0, 0), pipeline_mode=pl.Buffered(1)),
                  full(1, D_MODEL),
                  part(ru, D_FF), part(rd, D_MODEL)],
        out_specs=[prev(D_MODEL), prev(D_MODEL), part(ru, D_FF), part(rd, D_MODEL)],
        out_shape=[jax.ShapeDtypeStruct((t, D_MODEL), F32),
                   jax.ShapeDtypeStruct((t, D_MODEL), BF16),
                   jax.ShapeDtypeStruct((D_MODEL, D_FF), BF16),
                   jax.ShapeDtypeStruct((D_FF, D_MODEL), BF16)],
        scratch_shapes=[pltpu.VMEM((SUBLANES + SSM_CHUNK, SSM_CONV_DIM), F32),
                        pltpu.VMEM((SSM_D_STATE, SSM_D_INNER), F32),
                        pltpu.VMEM((tm, SSM_D_INNER), BF16),
                        pltpu.VMEM((tm, SSM_D_INNER), BF16)],
        compiler_params=pltpu.CompilerParams(
            dimension_semantics=("arbitrary",), vmem_limit_bytes=VMEM_LIMIT),
        name="ssd_outproj",
    )(xbc, z, dt, conv_w, conv_b, dt_bias, a_log, d_skip, norm_g,
      x2d, y_att, w_out16, mlp_g, w_up, w_down)


def _mlp_kernel(x1_ref, h2_ref, wu_ref, wd_ref, g_ref, o_ref):
    j = pl.program_id(1)

    @pl.when(j == 0)
    def _():
        o_ref[...] = x1_ref[...]

    u = jnp.maximum(_dot(h2_ref[...], wu_ref[...]), 0.0)
    o_ref[...] += _dot((u * u).astype(BF16), wd_ref[...])

    @pl.when(j == pl.num_programs(1) - 1)
    def _():
        x2 = o_ref[...]
        o_ref[...] = x2 * _rms_scale(x2) * g_ref[...]


def _mlp(x1, h2, w_up, w_down, g):
    t = x1.shape[0]
    tm, tf = TM_MLP, TF_MLP
    return pl.pallas_call(
        _mlp_kernel,
        grid=(t // tm, D_FF // tf),
        in_specs=[pl.BlockSpec((tm, D_MODEL), lambda i, j: (i, 0)),
                  pl.BlockSpec((tm, D_MODEL), lambda i, j: (i, 0)),
                  pl.BlockSpec((D_MODEL, tf), lambda i, j: (0, j)),
                  pl.BlockSpec((tf, D_MODEL), lambda i, j: (j, 0)),
                  pl.BlockSpec((1, D_MODEL), lambda i, j: (0, 0))],
        out_specs=pl.BlockSpec((tm, D_MODEL), lambda i, j: (i, 0)),
        out_shape=jax.ShapeDtypeStruct((t, D_MODEL), F32),
        compiler_params=pltpu.CompilerParams(
            dimension_semantics=("parallel", "arbitrary"), vmem_limit_bytes=VMEM_LIMIT),
        name="mlp",
    )(x1, h2, w_up, w_down, g)


def _row(v, width=None):
    v = v.astype(F32).reshape(1, -1)
    if width is not None and v.shape[1] < width:
        v = jnp.pad(v, ((0, 0), (0, width - v.shape[1])))
    return v


def kernel(x, mix_norm_g, w_in, conv_w, conv_b, dt_bias, A_log, D_skip, ssm_norm_g,
           attn_sinks, attn_out_norm_g, w_out, mlp_norm_g, w_up, w_down, final_norm_g):
    b, l, d = x.shape
    assert d == D_MODEL and l % SSM_CHUNK == 0 and l % TM_PROJ == 0
    assert w_in.shape[0] == 1, "one layer"
    x2d = x.reshape(b * l, d)

    wi = w_in[0]
    o_dt = SSM_D_INNER + SSM_CONV_DIM
    o_q = o_dt + SSM_N_HEADS
    wa = wi[:, :o_dt].astype(BF16)
    wb = wi[:, o_q:].astype(BF16)
    wdt = jnp.pad(wi[:, o_dt:o_q], ((0, 0), (0, DT_PAD - SSM_N_HEADS))).astype(BF16)

    z, xbc, dt, y_att, w_out16 = _inproj_attn(
        x2d, _row(mix_norm_g[0]), wa, wb, wdt,
        attn_sinks[0].astype(F32), _row(attn_out_norm_g[0]), w_out[0], l)

    x1, h2, w_up16, w_down16 = _ssd_outproj(
        x2d, xbc, z, dt, y_att,
        conv_w[0].astype(F32), _row(conv_b[0]),
        _row(dt_bias[0], DT_PAD), _row(A_log[0], DT_PAD),
        _row(jnp.repeat(D_skip[0], SSM_HEAD_DIM)), _row(ssm_norm_g[0]),
        w_out16, _row(mlp_norm_g[0]), w_up[0], w_down[0], l)
    out = _mlp(x1, h2, w_up16, w_down16, _row(final_norm_g))
    return out.reshape(b, l, d)
```

```python
import functools

import jax
import jax.numpy as jnp
from jax import lax
from jax.experimental import pallas as pl
from jax.experimental.pallas import tpu as pltpu

F32 = jnp.float32
BF16 = jnp.bfloat16

D_MODEL = 2048
SSM_D_INNER = 1024
SSM_HEAD_DIM = 64
SSM_N_HEADS = 16
SSM_N_GROUPS = 4
SSM_D_STATE = 128
SSM_CONV = 4
SSM_CHUNK = 128
SSM_GN = SSM_N_GROUPS * SSM_D_STATE
SSM_CONV_DIM = SSM_D_INNER + 2 * SSM_GN
SSM_GROUP_WIDTH = SSM_D_INNER // SSM_N_GROUPS
ATTN_WIDTH = 1024
ATTN_HEAD_DIM = 64
ATTN_N_HEADS = 16
ATTN_N_KV = 2
ATTN_KV_WIDTH = 2 * ATTN_N_KV * ATTN_HEAD_DIM
ATTN_BLOCK = 128
D_FF = 8192
EPS = 1e-5
LOG2E = 1.4426950408889634

LANES = 128
DT_PAD = LANES
SUBLANES = 8
MXU_WIDTH = 256
PROJ_ATTN_PATTERN = "abbbb"
VMEM_LIMIT = 56 * 1024 * 1024

SCAN_PROJ_PATTERN = "b" + 2 * "abaababaabaa" + "b"

TM_PROJ = 512
W_STAGE_ROWS = 128
TM_SSD = 256
TM_MLP = 512
TF_MLP = 1024


def _dot(a, b):
    return jnp.dot(a, b, preferred_element_type=F32)


def _split3(x):
    hi = x.astype(BF16)
    r = x - hi.astype(F32)
    mid = r.astype(BF16)
    lo = (r - mid.astype(F32)).astype(BF16)
    return hi, mid, lo


def _rms_scale(x):
    return lax.rsqrt(jnp.mean(x * x, axis=-1, keepdims=True) + EPS)


def _silu(x):
    hx = 0.5 * x
    return hx + hx * jnp.tanh(hx)


def _interleave(gen_a, gen_b, pattern):
    gens = {"a": gen_a, "b": gen_b}
    live = {"a": True, "b": True}
    while any(live.values()):
        for key in pattern:
            if live[key]:
                live[key] = next(gens[key], _DONE) is not _DONE


_DONE = object()


def _attn_block_stages(q_blk, kv_prev, kv_cur, first_block, sink_ref, emit):
    QB = ATTN_BLOCK
    D = ATTN_HEAD_DIM
    kv = jnp.concatenate([kv_prev, kv_cur], axis=0).astype(F32)
    k01 = kv[:, :2 * D]
    v01 = kv[:, 2 * D:]
    k_t = k01.T
    v01r = pltpu.roll(v01, D, axis=1)

    lane_kv = lax.broadcasted_iota(jnp.int32, (2 * QB, LANES), 1)
    first_kv = lane_kv < D
    ones_e = first_kv.astype(BF16)
    ones_o = 1 - ones_e
    zero_t = jnp.zeros((D, 2 * QB), BF16)

    i = lax.broadcasted_iota(jnp.int32, (QB, 2 * QB), 0)
    j = lax.broadcasted_iota(jnp.int32, (QB, 2 * QB), 1)
    valid = (j > i) & (j <= i + QB) & ((j >= QB) | jnp.logical_not(first_block))
    lane_q = lax.broadcasted_iota(jnp.int32, (QB, LANES), 1)
    first_q = lane_q < D

    per_kv = ATTN_N_HEADS // ATTN_N_KV // 2
    rhs, vb = [], []
    for g in range(ATTN_N_KV):
        kg_t = k_t[g * D:(g + 1) * D, :].astype(BF16)
        rhs.append(jnp.concatenate([jnp.concatenate([kg_t, zero_t], axis=1),
                                    jnp.concatenate([zero_t, kg_t], axis=1)], axis=0))
        if g == 0:
            v_e = jnp.where(first_kv, v01, 0.0)
            v_o = jnp.where(first_kv, 0.0, v01r)
        else:
            v_e = jnp.where(first_kv, v01r, 0.0)
            v_o = jnp.where(first_kv, 0.0, v01)
        vb.append(jnp.concatenate(
            [jnp.concatenate([v_e.astype(BF16), ones_e], axis=1),
             jnp.concatenate([v_o.astype(BF16), ones_o], axis=1)], axis=0))
    yield

    def scores(jp):
        s = _dot(q_blk(jp), rhs[jp // per_kv]) * (D ** -0.5)
        s_e = jnp.where(valid, s[:, :2 * QB], -jnp.inf)
        s_o = jnp.where(valid, s[:, 2 * QB:], -jnp.inf)
        sink_e = sink_ref[2 * jp]
        sink_o = sink_ref[2 * jp + 1]
        m_e = jnp.maximum(jnp.max(s_e, axis=-1, keepdims=True), sink_e)
        m_o = jnp.maximum(jnp.max(s_o, axis=-1, keepdims=True), sink_o)
        p = jnp.concatenate([jnp.exp(s_e - m_e), jnp.exp(s_o - m_o)], axis=1).astype(BF16)
        sink_term = jnp.where(first_q, jnp.exp(sink_e - m_e), jnp.exp(sink_o - m_o))
        return p, sink_term

    def values(jp, p, sink_term):
        o = _dot(p, vb[jp // per_kv])
        return o[:, :LANES] / (o[:, LANES:] + sink_term)

    n_pairs = ATTN_N_KV * per_kv
    outs = []
    pending = scores(0)
    yield
    for jp in range(1, n_pairs):
        nxt = scores(jp)
        yield
        outs.append(values(jp - 1, *pending))
        pending = nxt
        yield
    outs.append(values(n_pairs - 1, *pending))
    emit(jnp.concatenate(outs, axis=1))
    yield


def _inproj_attn_kernel(sink_ref, x_ref, g_ref, w_hbm, ng_ref, wo_ref,
                        z_ref, xbc_ref, dt_ref, yatt_ref, wo16_ref,
                        wa_ref, wb_ref, wdt_ref, stage, stage_sem,
                        cur_scr, prev_scr, kvtail_scr, *, tiles_per_seq):
    s = pl.program_id(0)
    n_tiles = pl.num_programs(0) - 1
    blocks = TM_PROJ // ATTN_BLOCK
    tail = slice(TM_PROJ - ATTN_BLOCK, TM_PROJ)

    wo16_ref[...] = wo_ref[...].astype(BF16)

    def stage_copy(r, slot):
        return pltpu.make_async_copy(w_hbm.at[0, pl.ds(r * W_STAGE_ROWS, W_STAGE_ROWS), :],
                                     stage.at[slot], stage_sem.at[slot])

    def load_weights():
        o_dt = SSM_D_INNER + SSM_CONV_DIM
        o_q = o_dt + SSM_N_HEADS
        n_chunks = D_MODEL // W_STAGE_ROWS
        lane = lax.broadcasted_iota(jnp.int32, (W_STAGE_ROWS, DT_PAD), 1)
        stage_copy(0, 0).start()
        for r in range(n_chunks):
            slot = r % 2
            if r + 1 < n_chunks:
                stage_copy(r + 1, 1 - slot).start()
            stage_copy(r, slot).wait()
            rows = slice(r * W_STAGE_ROWS, (r + 1) * W_STAGE_ROWS)
            wa_ref[rows, :] = stage[slot, :, :o_dt].astype(BF16)
            wb_ref[rows, :] = stage[slot, :, o_q:].astype(BF16)
            wdt_ref[rows, :] = jnp.where(lane < SSM_N_HEADS,
                                         stage[slot, :, o_dt:o_dt + DT_PAD], 0.0).astype(BF16)

    def project():
        x = x_ref[...]
        h = (x * _rms_scale(x) * g_ref[...]).astype(BF16)
        yield
        for w_ref, dsts in ((wa_ref, ((z_ref, 0, SSM_D_INNER), (xbc_ref, SSM_D_INNER, SSM_CONV_DIM))),
                            (wb_ref, ((cur_scr, 0, ATTN_WIDTH + ATTN_KV_WIDTH),))):
            for dst, base, width in dsts:
                for lo in range(0, width, MXU_WIDTH):
                    dst[:, lo:lo + MXU_WIDTH] = _dot(
                        h, w_ref[:, base + lo:base + lo + MXU_WIDTH]).astype(BF16)
                    yield
        dt_ref[...] = _dot(h, wdt_ref[...])
        yield

    def rotate_buffers():
        kvtail_scr[...] = prev_scr[tail, ATTN_WIDTH:]
        prev_scr[...] = cur_scr[...]

    def attend():
        seq_start = ((s - 1) % tiles_per_seq) == 0
        for r in range(blocks):
            lo = r * ATTN_BLOCK
            kv_cur = prev_scr[lo:lo + ATTN_BLOCK, ATTN_WIDTH:]
            if r == 0:
                kv_prev = kvtail_scr[...]
                first_block = seq_start
            else:
                kv_prev = prev_scr[lo - ATTN_BLOCK:lo, ATTN_WIDTH:]
                first_block = False
            q_blk = lambda jp, lo=lo: prev_scr[lo:lo + ATTN_BLOCK, jp * LANES:(jp + 1) * LANES]

            def emit(y, lo=lo):
                yatt_ref[lo:lo + ATTN_BLOCK, :] = (y * _rms_scale(y) * ng_ref[...]).astype(BF16)

            yield from _attn_block_stages(q_blk, kv_prev, kv_cur, first_block, sink_ref, emit)

    @pl.when(s == 0)
    def _():
        prev_scr[tail, ATTN_WIDTH:] = jnp.zeros((ATTN_BLOCK, ATTN_KV_WIDTH), BF16)
        load_weights()
        for _ in project():
            pass

    @pl.when((s > 0) & (s < n_tiles))
    def _():
        rotate_buffers()
        _interleave(project(), attend(), PROJ_ATTN_PATTERN)

    @pl.when(s == n_tiles)
    def _():
        rotate_buffers()
        for _ in attend():
            pass


def _inproj_attn(x2d, g, w_in, sinks, attn_g, w_out, seq_len):
    t = x2d.shape[0]
    tm = TM_PROJ
    n_tiles = t // tm
    cur = lambda width: pl.BlockSpec((tm, width), lambda i: (jnp.minimum(i, n_tiles - 1), 0))
    prev = lambda width: pl.BlockSpec((tm, width), lambda i: (jnp.maximum(i - 1, 0), 0))
    vec = lambda width: pl.BlockSpec((1, width), lambda i: (0, 0))
    wo_rows = pl.BlockSpec((D_MODEL // n_tiles, D_MODEL),
                           lambda i: (jnp.minimum(i, n_tiles - 1), 0))
    return pl.pallas_call(
        functools.partial(_inproj_attn_kernel, tiles_per_seq=seq_len // tm),
        grid=(n_tiles + 1,),
        in_specs=[
            pl.BlockSpec(memory_space=pltpu.SMEM),
            cur(D_MODEL),
            vec(D_MODEL),
            pl.BlockSpec(memory_space=pl.ANY),
            vec(ATTN_WIDTH),
            wo_rows,
        ],
        out_specs=[cur(SSM_D_INNER), cur(SSM_CONV_DIM), cur(DT_PAD), prev(ATTN_WIDTH), wo_rows],
        out_shape=[
            jax.ShapeDtypeStruct((t, SSM_D_INNER), BF16),
            jax.ShapeDtypeStruct((t, SSM_CONV_DIM), BF16),
            jax.ShapeDtypeStruct((t, DT_PAD), F32),
            jax.ShapeDtypeStruct((t, ATTN_WIDTH), BF16),
            jax.ShapeDtypeStruct((D_MODEL, D_MODEL), BF16),
        ],
        scratch_shapes=[pltpu.VMEM((D_MODEL, SSM_D_INNER + SSM_CONV_DIM), BF16),
                        pltpu.VMEM((D_MODEL, ATTN_WIDTH + ATTN_KV_WIDTH), BF16),
                        pltpu.VMEM((D_MODEL, DT_PAD), BF16),
                        pltpu.VMEM((2, W_STAGE_ROWS, w_in.shape[-1]), F32),
                        pltpu.SemaphoreType.DMA((2,)),
                        pltpu.VMEM((tm, ATTN_WIDTH + ATTN_KV_WIDTH), BF16),
                        pltpu.VMEM((tm, ATTN_WIDTH + ATTN_KV_WIDTH), BF16),
                        pltpu.VMEM((ATTN_BLOCK, ATTN_KV_WIDTH), BF16)],
        compiler_params=pltpu.CompilerParams(
            dimension_semantics=("arbitrary",), vmem_limit_bytes=VMEM_LIMIT),
        name="inproj_attn",
    )(sinks, x2d, g, w_in, attn_g, w_out)


def _ssd_chunk_stages(xbc, z, dt_raw, cw_ref, cb_ref, dtb_ref, alog_ref, dskip_ref, ng_ref,
                      cbuf, state, emit):
    Q = SSM_CHUNK
    heads_per_group = SSM_N_HEADS // SSM_N_GROUPS
    dtr = dt_raw + dtb_ref[...]
    dt = jnp.maximum(dtr, 0.0) + jnp.log1p(jnp.exp(-jnp.abs(dtr)))
    a = dt * (-LOG2E * jnp.exp(alog_ref[...]))

    row = lax.broadcasted_iota(jnp.int32, (Q, Q), 0)
    col = lax.broadcasted_iota(jnp.int32, (Q, Q), 1)
    causal = row >= col
    tri = causal.astype(BF16)
    a_hi, a_mid, a_lo = _split3(a)
    a_cs = _dot(tri, a_hi) + _dot(tri, a_mid) + _dot(tri, a_lo)
    a_cs_t = a_cs.T
    yield

    cur = xbc.astype(F32)
    cbuf[SUBLANES:SUBLANES + Q, :] = cur
    acc = cb_ref[...] + cw_ref[SSM_CONV - 1:SSM_CONV, :] * cur
    for k in range(SSM_CONV - 1):
        lo = SUBLANES - (SSM_CONV - 1) + k
        acc = acc + cw_ref[k:k + 1, :] * cbuf[lo:lo + Q, :]
    cbuf[0:SUBLANES, :] = cur[Q - SUBLANES:Q, :]
    xc = _silu(acc)
    xs = xc[:, :SSM_D_INNER]
    yield

    er = lax.broadcasted_iota(jnp.int32, (LANES, SSM_D_INNER), 0)
    ec = lax.broadcasted_iota(jnp.int32, (LANES, SSM_D_INNER), 1)
    expand = ((ec // SSM_HEAD_DIM) == er).astype(BF16)
    stacked = jnp.concatenate([dt, a_cs], axis=0)
    s_hi, s_mid, _ = _split3(stacked)
    ex = _dot(s_hi, expand) + _dot(s_mid, expand)
    dt_x = ex[:Q]
    acs_x = ex[Q:]
    acs_last = acs_x[Q - 1:Q, :]
    yield

    xd = xs * dt_x
    xd16 = xd.astype(BF16)
    xdd16 = (xd * jnp.exp2(acs_last - acs_x)).astype(BF16)
    exp_acs = jnp.exp2(acs_x)
    chunk_decay = jnp.exp2(acs_last)

    lane = lax.broadcasted_iota(jnp.int32, (Q, LANES), 1)
    first_half = lane < SSM_HEAD_DIM
    zero16 = jnp.zeros((Q, LANES), BF16)

    b16, c16, cb = [], [], []
    for g in range(SSM_N_GROUPS):
        b16.append(xc[:, SSM_D_INNER + g * SSM_D_STATE:
                      SSM_D_INNER + (g + 1) * SSM_D_STATE].astype(BF16))
        c16.append(xc[:, SSM_D_INNER + SSM_GN + g * SSM_D_STATE:
                      SSM_D_INNER + SSM_GN + (g + 1) * SSM_D_STATE].astype(BF16))
        cb.append(lax.dot_general(c16[g], b16[g], (((1,), (1,)), ((), ())),
                                  preferred_element_type=F32))
    yield

    y_off = []
    for g in range(SSM_N_GROUPS):
        gl = g * SSM_GROUP_WIDTH
        st = state[:, gl:gl + SSM_GROUP_WIDTH]
        y_off.append(_dot(c16[g], st.astype(BF16)) * exp_acs[:, gl:gl + SSM_GROUP_WIDTH])
        upd = lax.dot_general(b16[g], xdd16[:, gl:gl + SSM_GROUP_WIDTH],
                              (((0,), (0,)), ((), ())), preferred_element_type=F32)
        state[:, gl:gl + SSM_GROUP_WIDTH] = chunk_decay[:, gl:gl + SSM_GROUP_WIDTH] * st + upd
    yield

    y_parts = []
    for g in range(SSM_N_GROUPS):
        ms = []
        for r in range(heads_per_group):
            h = g * heads_per_group + r
            seg = a_cs[:, h:h + 1] - a_cs_t[h:h + 1, :]
            lmat = jnp.exp2(jnp.where(causal, seg, -jnp.inf))
            ms.append((cb[g] * lmat).astype(BF16))
        gl = g * SSM_GROUP_WIDTH
        yd = []
        for pr in range(2):
            xp = xd16[:, gl + pr * LANES:gl + (pr + 1) * LANES]
            rhs = jnp.concatenate([jnp.where(first_half, xp, zero16),
                                   jnp.where(first_half, zero16, xp)], axis=0)
            lhs = jnp.concatenate([ms[2 * pr], ms[2 * pr + 1]], axis=1)
            yd.append(_dot(lhs, rhs))
        y_parts.append(jnp.concatenate(yd, axis=1) + y_off[g])
        if g % 2 == 1:
            yield

    y = jnp.concatenate(y_parts, axis=1) + xs * dskip_ref[...]
    y = y * _silu(z.astype(F32))
    outs = []
    for g in range(SSM_N_GROUPS):
        yg = y[:, g * SSM_GROUP_WIDTH:(g + 1) * SSM_GROUP_WIDTH]
        outs.append(yg * _rms_scale(yg))
    emit(jnp.concatenate(outs, axis=1) * ng_ref[...])
    yield


def _ssd_outproj_kernel(xbc_ref, z_ref, dt_ref, cw_ref, cb_ref, dtb_ref, alog_ref, dskip_ref,
                        ng_ref, x_ref, ya_ref, wo_ref, g2_ref, wu_ref, wd_ref,
                        x1_ref, h2_ref, wu16_ref, wd16_ref,
                        cbuf, state, ys_cur, ys_prev, *, tiles_per_seq):
    s = pl.program_id(0)
    n_tiles = pl.num_programs(0) - 1
    Q = SSM_CHUNK

    wu16_ref[...] = wu_ref[...].astype(BF16)
    wd16_ref[...] = wd_ref[...].astype(BF16)

    @pl.when((s % tiles_per_seq) == 0)
    def _():
        cbuf[0:SUBLANES, :] = jnp.zeros((SUBLANES, SSM_CONV_DIM), F32)
        state[...] = jnp.zeros_like(state)

    def scan():
        for c in range(TM_SSD // Q):
            lo = c * Q

            def emit(y, lo=lo):
                ys_cur[lo:lo + Q, :] = y.astype(BF16)

            yield from _ssd_chunk_stages(
                xbc_ref[lo:lo + Q, :], z_ref[lo:lo + Q, :], dt_ref[lo:lo + Q, :],
                cw_ref, cb_ref, dtb_ref, alog_ref, dskip_ref, ng_ref, cbuf, state, emit)

    def project():
        y = jnp.concatenate([ys_prev[...], ya_ref[...]], axis=1)
        yield
        for lo in range(0, D_MODEL, MXU_WIDTH):
            x1_ref[:, lo:lo + MXU_WIDTH] = (x_ref[:, lo:lo + MXU_WIDTH]
                                            + _dot(y, wo_ref[:, lo:lo + MXU_WIDTH]))
            yield
        x1 = x1_ref[...]
        h2_ref[...] = (x1 * _rms_scale(x1) * g2_ref[...]).astype(BF16)
        yield

    @pl.when(s == 0)
    def _():
        for _ in scan():
            pass

    @pl.when((s > 0) & (s < n_tiles))
    def _():
        ys_prev[...] = ys_cur[...]
        _interleave(scan(), project(), SCAN_PROJ_PATTERN)

    @pl.when(s == n_tiles)
    def _():
        ys_prev[...] = ys_cur[...]
        for _ in project():
            pass


def _ssd_outproj(x2d, xbc, z, dt, y_att, conv_w, conv_b, dt_bias, a_log, d_skip, norm_g,
                 w_out16, mlp_g, w_up, w_down, seq_len):
    t = x2d.shape[0]
    tm = TM_SSD
    n_tiles = t // tm
    cur = lambda width: pl.BlockSpec((tm, width), lambda i: (jnp.minimum(i, n_tiles - 1), 0))
    prev = lambda width: pl.BlockSpec((tm, width), lambda i: (jnp.maximum(i - 1, 0), 0))
    full = lambda r, width: pl.BlockSpec((r, width), lambda i: (0, 0))
    part = lambda r, width: pl.BlockSpec((r, width), lambda i: (jnp.minimum(i, n_tiles - 1), 0))
    ru, rd = D_MODEL // n_tiles, D_FF // n_tiles
    return pl.pallas_call(
        functools.partial(_ssd_outproj_kernel, tiles_per_seq=seq_len // tm),
        grid=(n_tiles + 1,),
        in_specs=[cur(SSM_CONV_DIM), cur(SSM_D_INNER), cur(DT_PAD),
                  full(SSM_CONV, SSM_CONV_DIM), full(1, SSM_CONV_DIM),
                  full(1, DT_PAD), full(1, DT_PAD),
                  full(1, SSM_D_INNER), full(1, SSM_D_INNER),
                  prev(D_MODEL), prev(ATTN_WIDTH),
                  pl.BlockSpec((D_MODEL, D_MODEL), lambda i: (0, 0), pipeline_mode=pl.Buffered(1)),
                  full(1, D_MODEL),
                  part(ru, D_FF), part(rd, D_MODEL)],
        out_specs=[prev(D_MODEL), prev(D_MODEL), part(ru, D_FF), part(rd, D_MODEL)],
        out_shape=[jax.ShapeDtypeStruct((t, D_MODEL), F32),
                   jax.ShapeDtypeStruct((t, D_MODEL), BF16),
                   jax.ShapeDtypeStruct((D_MODEL, D_FF), BF16),
                   jax.ShapeDtypeStruct((D_FF, D_MODEL), BF16)],
        scratch_shapes=[pltpu.VMEM((SUBLANES + SSM_CHUNK, SSM_CONV_DIM), F32),
                        pltpu.VMEM((SSM_D_STATE, SSM_D_INNER), F32),
                        pltpu.VMEM((tm, SSM_D_INNER), BF16),
                        pltpu.VMEM((tm, SSM_D_INNER), BF16)],
        compiler_params=pltpu.CompilerParams(
            dimension_semantics=("arbitrary",), vmem_limit_bytes=VMEM_LIMIT),
        name="ssd_outproj",
    )(xbc, z, dt, conv_w, conv_b, dt_bias, a_log, d_skip, norm_g,
      x2d, y_att, w_out16, mlp_g, w_up, w_down)


def _mlp_kernel(x1_ref, h2_ref, wu_ref, wd_ref, g_ref, o_ref):
    j = pl.program_id(1)

    @pl.when(j == 0)
    def _():
        o_ref[...] = x1_ref[...]

    u = jnp.maximum(_dot(h2_ref[...], wu_ref[...]), 0.0)
    o_ref[...] += _dot((u * u).astype(BF16), wd_ref[...])

    @pl.when(j == pl.num_programs(1) - 1)
    def _():
        x2 = o_ref[...]
        o_ref[...] = x2 * _rms_scale(x2) * g_ref[...]


def _mlp(x1, h2, w_up, w_down, g):
    t = x1.shape[0]
    tm, tf = TM_MLP, TF_MLP
    return pl.pallas_call(
        _mlp_kernel,
        grid=(t // tm, D_FF // tf),
        in_specs=[pl.BlockSpec((tm, D_MODEL), lambda i, j: (i, 0)),
                  pl.BlockSpec((tm, D_MODEL), lambda i, j: (i, 0)),
                  pl.BlockSpec((D_MODEL, tf), lambda i, j: (0, j)),
                  pl.BlockSpec((tf, D_MODEL), lambda i, j: (j, 0)),
                  pl.BlockSpec((1, D_MODEL), lambda i, j: (0, 0))],
        out_specs=pl.BlockSpec((tm, D_MODEL), lambda i, j: (i, 0)),
        out_shape=jax.ShapeDtypeStruct((t, D_MODEL), F32),
        compiler_params=pltpu.CompilerParams(
            dimension_semantics=("parallel", "arbitrary"), vmem_limit_bytes=VMEM_LIMIT),
        name="mlp",
    )(x1, h2, w_up, w_down, g)


def _row(v, width=None):
    v = v.astype(F32).reshape(1, -1)
    if width is not None and v.shape[1] < width:
        v = jnp.pad(v, ((0, 0), (0, width - v.shape[1])))
    return v


def kernel(x, mix_norm_g, w_in, conv_w, conv_b, dt_bias, A_log, D_skip, ssm_norm_g,
           attn_sinks, attn_out_norm_g, w_out, mlp_norm_g, w_up, w_down, final_norm_g):
    b, l, d = x.shape
    assert d == D_MODEL and l % SSM_CHUNK == 0 and l % TM_PROJ == 0
    assert w_in.shape[0] == 1, "one layer"
    x2d = x.reshape(b * l, d)

    z, xbc, dt, y_att, w_out16 = _inproj_attn(
        x2d, _row(mix_norm_g[0]), w_in.astype(F32),
        attn_sinks[0].astype(F32), _row(attn_out_norm_g[0]), w_out[0], l)

    x1, h2, w_up16, w_down16 = _ssd_outproj(
        x2d, xbc, z, dt, y_att,
        conv_w[0].astype(F32), _row(conv_b[0]),
        _row(dt_bias[0], DT_PAD), _row(A_log[0], DT_PAD),
        _row(jnp.repeat(D_skip[0], SSM_HEAD_DIM)), _row(ssm_norm_g[0]),
        w_out16, _row(mlp_norm_g[0]), w_up[0], w_down[0], l)
    out = _mlp(x1, h2, w_up16, w_down16, _row(final_norm_g))
    return out.reshape(b, l, d)
```

```python
import functools

import jax
import jax.numpy as jnp
from jax import lax
from jax.experimental import pallas as pl
from jax.experimental.pallas import tpu as pltpu

F32 = jnp.float32
BF16 = jnp.bfloat16

D_MODEL = 2048
SSM_D_INNER = 1024
SSM_HEAD_DIM = 64
SSM_N_HEADS = 16
SSM_N_GROUPS = 4
SSM_D_STATE = 128
SSM_CONV = 4
SSM_CHUNK = 128
SSM_GN = SSM_N_GROUPS * SSM_D_STATE
SSM_CONV_DIM = SSM_D_INNER + 2 * SSM_GN
SSM_GROUP_WIDTH = SSM_D_INNER // SSM_N_GROUPS
ATTN_WIDTH = 1024
ATTN_HEAD_DIM = 64
ATTN_N_HEADS = 16
ATTN_N_KV = 2
ATTN_KV_WIDTH = 2 * ATTN_N_KV * ATTN_HEAD_DIM
ATTN_BLOCK = 128
D_FF = 8192
EPS = 1e-5
LOG2E = 1.4426950408889634

LANES = 128
DT_PAD = LANES
SUBLANES = 8
MXU_WIDTH = 256
PROJ_ATTN_PATTERN = "abbbb"
VMEM_LIMIT = 56 * 1024 * 1024

SCAN_PROJ_PATTERN = "b" + 2 * "abaababaabaa" + "b"

TM_PROJ = 512
W_STAGE_ROWS = 208
TM_SSD = 256
TM_MLP = 512
TF_MLP = 1024


def _dot(a, b):
    return jnp.dot(a, b, preferred_element_type=F32)


def _split3(x):
    hi = x.astype(BF16)
    r = x - hi.astype(F32)
    mid = r.astype(BF16)
    lo = (r - mid.astype(F32)).astype(BF16)
    return hi, mid, lo


def _rms_scale(x):
    return lax.rsqrt(jnp.mean(x * x, axis=-1, keepdims=True) + EPS)


def _silu(x):
    hx = 0.5 * x
    return hx + hx * jnp.tanh(hx)


def _interleave(gen_a, gen_b, pattern):
    gens = {"a": gen_a, "b": gen_b}
    live = {"a": True, "b": True}
    while any(live.values()):
        for key in pattern:
            if live[key]:
                live[key] = next(gens[key], _DONE) is not _DONE


_DONE = object()


def _attn_block_stages(q_blk, kv_prev, kv_cur, first_block, sink_ref, emit):
    QB = ATTN_BLOCK
    D = ATTN_HEAD_DIM
    kv = jnp.concatenate([kv_prev, kv_cur], axis=0).astype(F32)
    k01 = kv[:, :2 * D]
    v01 = kv[:, 2 * D:]
    k_t = k01.T
    v01r = pltpu.roll(v01, D, axis=1)

    lane_kv = lax.broadcasted_iota(jnp.int32, (2 * QB, LANES), 1)
    first_kv = lane_kv < D
    ones_e = first_kv.astype(BF16)
    ones_o = 1 - ones_e
    zero_t = jnp.zeros((D, 2 * QB), BF16)

    i = lax.broadcasted_iota(jnp.int32, (QB, 2 * QB), 0)
    j = lax.broadcasted_iota(jnp.int32, (QB, 2 * QB), 1)
    valid = (j > i) & (j <= i + QB) & ((j >= QB) | jnp.logical_not(first_block))
    lane_q = lax.broadcasted_iota(jnp.int32, (QB, LANES), 1)
    first_q = lane_q < D

    per_kv = ATTN_N_HEADS // ATTN_N_KV // 2
    rhs, vb = [], []
    for g in range(ATTN_N_KV):
        kg_t = k_t[g * D:(g + 1) * D, :].astype(BF16)
        rhs.append(jnp.concatenate([jnp.concatenate([kg_t, zero_t], axis=1),
                                    jnp.concatenate([zero_t, kg_t], axis=1)], axis=0))
        if g == 0:
            v_e = jnp.where(first_kv, v01, 0.0)
            v_o = jnp.where(first_kv, 0.0, v01r)
        else:
            v_e = jnp.where(first_kv, v01r, 0.0)
            v_o = jnp.where(first_kv, 0.0, v01)
        vb.append(jnp.concatenate(
            [jnp.concatenate([v_e.astype(BF16), ones_e], axis=1),
             jnp.concatenate([v_o.astype(BF16), ones_o], axis=1)], axis=0))
    yield

    def scores(jp):
        s = _dot(q_blk(jp), rhs[jp // per_kv]) * (D ** -0.5)
        s_e = jnp.where(valid, s[:, :2 * QB], -jnp.inf)
        s_o = jnp.where(valid, s[:, 2 * QB:], -jnp.inf)
        sink_e = sink_ref[2 * jp]
        sink_o = sink_ref[2 * jp + 1]
        m_e = jnp.maximum(jnp.max(s_e, axis=-1, keepdims=True), sink_e)
        m_o = jnp.maximum(jnp.max(s_o, axis=-1, keepdims=True), sink_o)
        p = jnp.concatenate([jnp.exp(s_e - m_e), jnp.exp(s_o - m_o)], axis=1).astype(BF16)
        sink_term = jnp.where(first_q, jnp.exp(sink_e - m_e), jnp.exp(sink_o - m_o))
        return p, sink_term

    def values(jp, p, sink_term):
        o = _dot(p, vb[jp // per_kv])
        return o[:, :LANES] / (o[:, LANES:] + sink_term)

    n_pairs = ATTN_N_KV * per_kv
    outs = []
    pending = scores(0)
    yield
    for jp in range(1, n_pairs):
        nxt = scores(jp)
        yield
        outs.append(values(jp - 1, *pending))
        pending = nxt
        yield
    outs.append(values(n_pairs - 1, *pending))
    emit(jnp.concatenate(outs, axis=1))
    yield


def _inproj_attn_kernel(sink_ref, x_ref, g_ref, wt_hbm, ng_ref, wo_ref,
                        z_ref, xbc_ref, dt_ref, yatt_ref, wo16_ref,
                        wt_ref, stage, stage_sem,
                        cur_scr, prev_scr, kvtail_scr, *, tiles_per_seq):
    s = pl.program_id(0)
    n_tiles = pl.num_programs(0) - 1
    blocks = TM_PROJ // ATTN_BLOCK
    tail = slice(TM_PROJ - ATTN_BLOCK, TM_PROJ)

    wo16_ref[...] = wo_ref[...].astype(BF16)

    def stage_copy(r, slot):
        return pltpu.make_async_copy(wt_hbm.at[0, pl.ds(r * W_STAGE_ROWS, W_STAGE_ROWS), :],
                                     stage.at[slot], stage_sem.at[slot])

    def load_weights():
        n_chunks = wt_ref.shape[0] // W_STAGE_ROWS
        stage_copy(0, 0).start()
        for r in range(n_chunks):
            slot = r % 2
            if r + 1 < n_chunks:
                stage_copy(r + 1, 1 - slot).start()
            stage_copy(r, slot).wait()
            wt_ref[r * W_STAGE_ROWS:(r + 1) * W_STAGE_ROWS, :] = stage[slot].astype(BF16)

    def project():
        x = x_ref[...]
        h = (x * _rms_scale(x) * g_ref[...]).astype(BF16)
        yield

        def cols(lo, width):
            return lax.dot_general(h, wt_ref[lo:lo + width, :], (((1,), (1,)), ((), ())),
                                   preferred_element_type=F32)

        o_dt = SSM_D_INNER + SSM_CONV_DIM
        o_q = o_dt + SSM_N_HEADS
        for dst, base, width in ((z_ref, 0, SSM_D_INNER), (xbc_ref, SSM_D_INNER, SSM_CONV_DIM),
                                 (cur_scr, o_q, ATTN_WIDTH + ATTN_KV_WIDTH)):
            for lo in range(0, width, MXU_WIDTH):
                dst[:, lo:lo + MXU_WIDTH] = cols(base + lo, MXU_WIDTH).astype(BF16)
                yield
        lane = lax.broadcasted_iota(jnp.int32, (TM_PROJ, DT_PAD), 1)
        dt_ref[...] = jnp.where(lane < SSM_N_HEADS, cols(o_dt, DT_PAD), 0.0)
        yield

    def rotate_buffers():
        kvtail_scr[...] = prev_scr[tail, ATTN_WIDTH:]
        prev_scr[...] = cur_scr[...]

    def attend():
        seq_start = ((s - 1) % tiles_per_seq) == 0
        for r in range(blocks):
            lo = r * ATTN_BLOCK
            kv_cur = prev_scr[lo:lo + ATTN_BLOCK, ATTN_WIDTH:]
            if r == 0:
                kv_prev = kvtail_scr[...]
                first_block = seq_start
            else:
                kv_prev = prev_scr[lo - ATTN_BLOCK:lo, ATTN_WIDTH:]
                first_block = False
            q_blk = lambda jp, lo=lo: prev_scr[lo:lo + ATTN_BLOCK, jp * LANES:(jp + 1) * LANES]

            def emit(y, lo=lo):
                yatt_ref[lo:lo + ATTN_BLOCK, :] = (y * _rms_scale(y) * ng_ref[...]).astype(BF16)

            yield from _attn_block_stages(q_blk, kv_prev, kv_cur, first_block, sink_ref, emit)

    @pl.when(s == 0)
    def _():
        prev_scr[tail, ATTN_WIDTH:] = jnp.zeros((ATTN_BLOCK, ATTN_KV_WIDTH), BF16)
        load_weights()
        for _ in project():
            pass

    @pl.when((s > 0) & (s < n_tiles))
    def _():
        rotate_buffers()
        _interleave(project(), attend(), PROJ_ATTN_PATTERN)

    @pl.when(s == n_tiles)
    def _():
        rotate_buffers()
        for _ in attend():
            pass


def _inproj_attn(x2d, g, w_in_t, sinks, attn_g, w_out, seq_len):
    t = x2d.shape[0]
    n_cols = w_in_t.shape[1]
    assert n_cols % W_STAGE_ROWS == 0 and w_in_t.shape[2] == D_MODEL
    tm = TM_PROJ
    n_tiles = t // tm
    cur = lambda width: pl.BlockSpec((tm, width), lambda i: (jnp.minimum(i, n_tiles - 1), 0))
    prev = lambda width: pl.BlockSpec((tm, width), lambda i: (jnp.maximum(i - 1, 0), 0))
    vec = lambda width: pl.BlockSpec((1, width), lambda i: (0, 0))
    wo_rows = pl.BlockSpec((D_MODEL // n_tiles, D_MODEL),
                           lambda i: (jnp.minimum(i, n_tiles - 1), 0))
    return pl.pallas_call(
        functools.partial(_inproj_attn_kernel, tiles_per_seq=seq_len // tm),
        grid=(n_tiles + 1,),
        in_specs=[
            pl.BlockSpec(memory_space=pltpu.SMEM),
            cur(D_MODEL),
            vec(D_MODEL),
            pl.BlockSpec(memory_space=pl.ANY),
            vec(ATTN_WIDTH),
            wo_rows,
        ],
        out_specs=[cur(SSM_D_INNER), cur(SSM_CONV_DIM), cur(DT_PAD), prev(ATTN_WIDTH), wo_rows],
        out_shape=[
            jax.ShapeDtypeStruct((t, SSM_D_INNER), BF16),
            jax.ShapeDtypeStruct((t, SSM_CONV_DIM), BF16),
            jax.ShapeDtypeStruct((t, DT_PAD), F32),
            jax.ShapeDtypeStruct((t, ATTN_WIDTH), BF16),
            jax.ShapeDtypeStruct((D_MODEL, D_MODEL), BF16),
        ],
        scratch_shapes=[pltpu.VMEM((n_cols, D_MODEL), BF16),
                        pltpu.VMEM((2, W_STAGE_ROWS, D_MODEL), F32),
                        pltpu.SemaphoreType.DMA((2,)),
                        pltpu.VMEM((tm, ATTN_WIDTH + ATTN_KV_WIDTH), BF16),
                        pltpu.VMEM((tm, ATTN_WIDTH + ATTN_KV_WIDTH), BF16),
                        pltpu.VMEM((ATTN_BLOCK, ATTN_KV_WIDTH), BF16)],
        compiler_params=pltpu.CompilerParams(
            dimension_semantics=("arbitrary",), vmem_limit_bytes=VMEM_LIMIT),
        name="inproj_attn",
    )(sinks, x2d, g, w_in_t, attn_g, w_out)


def _ssd_chunk_stages(xbc, z, dt_raw, cw_ref, cb_ref, dtb_ref, alog_ref, dskip_ref, ng_ref,
                      cbuf, state, emit):
    Q = SSM_CHUNK
    heads_per_group = SSM_N_HEADS // SSM_N_GROUPS
    dtr = dt_raw + dtb_ref[...]
    dt = jnp.maximum(dtr, 0.0) + jnp.log1p(jnp.exp(-jnp.abs(dtr)))
    a = dt * (-LOG2E * jnp.exp(alog_ref[...]))

    row = lax.broadcasted_iota(jnp.int32, (Q, Q), 0)
    col = lax.broadcasted_iota(jnp.int32, (Q, Q), 1)
    causal = row >= col
    tri = causal.astype(BF16)
    a_hi, a_mid, a_lo = _split3(a)
    a_cs = _dot(tri, a_hi) + _dot(tri, a_mid) + _dot(tri, a_lo)
    a_cs_t = a_cs.T
    yield

    cur = xbc.astype(F32)
    cbuf[SUBLANES:SUBLANES + Q, :] = cur
    acc = cb_ref[...] + cw_ref[SSM_CONV - 1:SSM_CONV, :] * cur
    for k in range(SSM_CONV - 1):
        lo = SUBLANES - (SSM_CONV - 1) + k
        acc = acc + cw_ref[k:k + 1, :] * cbuf[lo:lo + Q, :]
    cbuf[0:SUBLANES, :] = cur[Q - SUBLANES:Q, :]
    xc = _silu(acc)
    xs = xc[:, :SSM_D_INNER]
    yield

    er = lax.broadcasted_iota(jnp.int32, (LANES, SSM_D_INNER), 0)
    ec = lax.broadcasted_iota(jnp.int32, (LANES, SSM_D_INNER), 1)
    expand = ((ec // SSM_HEAD_DIM) == er).astype(BF16)
    stacked = jnp.concatenate([dt, a_cs], axis=0)
    s_hi, s_mid, _ = _split3(stacked)
    ex = _dot(s_hi, expand) + _dot(s_mid, expand)
    dt_x = ex[:Q]
    acs_x = ex[Q:]
    acs_last = acs_x[Q - 1:Q, :]
    yield

    xd = xs * dt_x
    xd16 = xd.astype(BF16)
    xdd16 = (xd * jnp.exp2(acs_last - acs_x)).astype(BF16)
    exp_acs = jnp.exp2(acs_x)
    chunk_decay = jnp.exp2(acs_last)

    lane = lax.broadcasted_iota(jnp.int32, (Q, LANES), 1)
    first_half = lane < SSM_HEAD_DIM
    zero16 = jnp.zeros((Q, LANES), BF16)

    b16, c16, cb = [], [], []
    for g in range(SSM_N_GROUPS):
        b16.append(xc[:, SSM_D_INNER + g * SSM_D_STATE:
                      SSM_D_INNER + (g + 1) * SSM_D_STATE].astype(BF16))
        c16.append(xc[:, SSM_D_INNER + SSM_GN + g * SSM_D_STATE:
                      SSM_D_INNER + SSM_GN + (g + 1) * SSM_D_STATE].astype(BF16))
        cb.append(lax.dot_general(c16[g], b16[g], (((1,), (1,)), ((), ())),
                                  preferred_element_type=F32))
    yield

    y_off = []
    for g in range(SSM_N_GROUPS):
        gl = g * SSM_GROUP_WIDTH
        st = state[:, gl:gl + SSM_GROUP_WIDTH]
        y_off.append(_dot(c16[g], st.astype(BF16)) * exp_acs[:, gl:gl + SSM_GROUP_WIDTH])
        upd = lax.dot_general(b16[g], xdd16[:, gl:gl + SSM_GROUP_WIDTH],
                              (((0,), (0,)), ((), ())), preferred_element_type=F32)
        state[:, gl:gl + SSM_GROUP_WIDTH] = chunk_decay[:, gl:gl + SSM_GROUP_WIDTH] * st + upd
    yield

    y_parts = []
    for g in range(SSM_N_GROUPS):
        ms = []
        for r in range(heads_per_group):
            h = g * heads_per_group + r
            seg = a_cs[:, h:h + 1] - a_cs_t[h:h + 1, :]
            lmat = jnp.exp2(jnp.where(causal, seg, -jnp.inf))
            ms.append((cb[g] * lmat).astype(BF16))
        gl = g * SSM_GROUP_WIDTH
        yd = []
        for pr in range(2):
            xp = xd16[:, gl + pr * LANES:gl + (pr + 1) * LANES]
            rhs = jnp.concatenate([jnp.where(first_half, xp, zero16),
                                   jnp.where(first_half, zero16, xp)], axis=0)
            lhs = jnp.concatenate([ms[2 * pr], ms[2 * pr + 1]], axis=1)
            yd.append(_dot(lhs, rhs))
        y_parts.append(jnp.concatenate(yd, axis=1) + y_off[g])
        if g % 2 == 1:
            yield

    y = jnp.concatenate(y_parts, axis=1) + xs * dskip_ref[...]
    y = y * _silu(z.astype(F32))
    outs = []
    for g in range(SSM_N_GROUPS):
        yg = y[:, g * SSM_GROUP_WIDTH:(g + 1) * SSM_GROUP_WIDTH]
        outs.append(yg * _rms_scale(yg))
    emit(jnp.concatenate(outs, axis=1) * ng_ref[...])
    yield


def _ssd_outproj_kernel(xbc_ref, z_ref, dt_ref, cw_ref, cb_ref, dtb_ref, alog_ref, dskip_ref,
                        ng_ref, x_ref, ya_ref, wo_ref, g2_ref, wu_ref, wd_ref,
                        x1_ref, h2_ref, wu16_ref, wd16_ref,
                        cbuf, state, ys_cur, ys_prev, *, tiles_per_seq):
    s = pl.program_id(0)
    n_tiles = pl.num_programs(0) - 1
    Q = SSM_CHUNK

    wu16_ref[...] = wu_ref[...].astype(BF16)
    wd16_ref[...] = wd_ref[...].astype(BF16)

    @pl.when((s % tiles_per_seq) == 0)
    def _():
        cbuf[0:SUBLANES, :] = jnp.zeros((SUBLANES, SSM_CONV_DIM), F32)
        state[...] = jnp.zeros_like(state)

    def scan():
        for c in range(TM_SSD // Q):
            lo = c * Q

            def emit(y, lo=lo):
                ys_cur[lo:lo + Q, :] = y.astype(BF16)

            yield from _ssd_chunk_stages(
                xbc_ref[lo:lo + Q, :], z_ref[lo:lo + Q, :], dt_ref[lo:lo + Q, :],
                cw_ref, cb_ref, dtb_ref, alog_ref, dskip_ref, ng_ref, cbuf, state, emit)

    def project():
        y = jnp.concatenate([ys_prev[...], ya_ref[...]], axis=1)
        yield
        for lo in range(0, D_MODEL, MXU_WIDTH):
            x1_ref[:, lo:lo + MXU_WIDTH] = (x_ref[:, lo:lo + MXU_WIDTH]
                                            + _dot(y, wo_ref[:, lo:lo + MXU_WIDTH]))
            yield
        x1 = x1_ref[...]
        h2_ref[...] = (x1 * _rms_scale(x1) * g2_ref[...]).astype(BF16)
        yield

    @pl.when(s == 0)
    def _():
        for _ in scan():
            pass

    @pl.when((s > 0) & (s < n_tiles))
    def _():
        ys_prev[...] = ys_cur[...]
        _interleave(scan(), project(), SCAN_PROJ_PATTERN)

    @pl.when(s == n_tiles)
    def _():
        ys_prev[...] = ys_cur[...]
        for _ in project():
            pass


def _ssd_outproj(x2d, xbc, z, dt, y_att, conv_w, conv_b, dt_bias, a_log, d_skip, norm_g,
                 w_out16, mlp_g, w_up, w_down, seq_len):
    t = x2d.shape[0]
    tm = TM_SSD
    n_tiles = t // tm
    cur = lambda width: pl.BlockSpec((tm, width), lambda i: (jnp.minimum(i, n_tiles - 1), 0))
    prev = lambda width: pl.BlockSpec((tm, width), lambda i: (jnp.maximum(i - 1, 0), 0))
    full = lambda r, width: pl.BlockSpec((r, width), lambda i: (0, 0))
    part = lambda r, width: pl.BlockSpec((r, width), lambda i: (jnp.minimum(i, n_tiles - 1), 0))
    ru, rd = D_MODEL // n_tiles, D_FF // n_tiles
    return pl.pallas_call(
        functools.partial(_ssd_outproj_kernel, tiles_per_seq=seq_len // tm),
        grid=(n_tiles + 1,),
        in_specs=[cur(SSM_CONV_DIM), cur(SSM_D_INNER), cur(DT_PAD),
                  full(SSM_CONV, SSM_CONV_DIM), full(1, SSM_CONV_DIM),
                  full(1, DT_PAD), full(1, DT_PAD),
                  full(1, SSM_D_INNER), full(1, SSM_D_INNER),
                  prev(D_MODEL), prev(ATTN_WIDTH),
                  pl.BlockSpec((D_MODEL, D_MODEL), lambda i: (0, 0), pipeline_mode=pl.Buffered(1)),
                  full(1, D_MODEL),
                  part(ru, D_FF), part(rd, D_MODEL)],
        out_specs=[prev(D_MODEL), prev(D_MODEL), part(ru, D_FF), part(rd, D_MODEL)],
        out_shape=[jax.ShapeDtypeStruct((t, D_MODEL), F32),
                   jax.ShapeDtypeStruct((t, D_MODEL), BF16),
                   jax.ShapeDtypeStruct((D_MODEL, D_FF), BF16),
                   jax.ShapeDtypeStruct((D_FF, D_MODEL), BF16)],
        scratch_shapes=[pltpu.VMEM((SUBLANES + SSM_CHUNK, SSM_CONV_DIM), F32),
                        pltpu.VMEM((SSM_D_STATE, SSM_D_INNER), F32),
                        pltpu.VMEM((tm, SSM_D_INNER), BF16),
                        pltpu.VMEM((tm, SSM_D_INNER), BF16)],
        compiler_params=pltpu.CompilerParams(
            dimension_semantics=("arbitrary",), vmem_limit_bytes=VMEM_LIMIT),
        name="ssd_outproj",
    )(xbc, z, dt, conv_w, conv_b, dt_bias, a_log, d_skip, norm_g,
      x2d, y_att, w_out16, mlp_g, w_up, w_down)


def _mlp_kernel(x1_ref, h2_ref, wu_ref, wd_ref, g_ref, o_ref):
    j = pl.program_id(1)

    @pl.when(j == 0)
    def _():
        o_ref[...] = x1_ref[...]

    u = jnp.maximum(_dot(h2_ref[...], wu_ref[...]), 0.0)
    o_ref[...] += _dot((u * u).astype(BF16), wd_ref[...])

    @pl.when(j == pl.num_programs(1) - 1)
    def _():
        x2 = o_ref[...]
        o_ref[...] = x2 * _rms_scale(x2) * g_ref[...]


def _mlp(x1, h2, w_up, w_down, g):
    t = x1.shape[0]
    tm, tf = TM_MLP, TF_MLP
    return pl.pallas_call(
        _mlp_kernel,
        grid=(t // tm, D_FF // tf),
        in_specs=[pl.BlockSpec((tm, D_MODEL), lambda i, j: (i, 0)),
                  pl.BlockSpec((tm, D_MODEL), lambda i, j: (i, 0)),
                  pl.BlockSpec((D_MODEL, tf), lambda i, j: (0, j)),
                  pl.BlockSpec((tf, D_MODEL), lambda i, j: (j, 0)),
                  pl.BlockSpec((1, D_MODEL), lambda i, j: (0, 0))],
        out_specs=pl.BlockSpec((tm, D_MODEL), lambda i, j: (i, 0)),
        out_shape=jax.ShapeDtypeStruct((t, D_MODEL), F32),
        compiler_params=pltpu.CompilerParams(
            dimension_semantics=("parallel", "arbitrary"), vmem_limit_bytes=VMEM_LIMIT),
        name="mlp",
    )(x1, h2, w_up, w_down, g)


def _row(v, width=None):
    v = v.astype(F32).reshape(1, -1)
    if width is not None and v.shape[1] < width:
        v = jnp.pad(v, ((0, 0), (0, width - v.shape[1])))
    return v


def kernel(x, mix_norm_g, w_in, conv_w, conv_b, dt_bias, A_log, D_skip, ssm_norm_g,
           attn_sinks, attn_out_norm_g, w_out, mlp_norm_g, w_up, w_down, final_norm_g):
    b, l, d = x.shape
    assert d == D_MODEL and l % SSM_CHUNK == 0 and l % TM_PROJ == 0
    assert w_in.shape[0] == 1, "one layer"
    x2d = x.reshape(b * l, d)

    z, xbc, dt, y_att, w_out16 = _inproj_attn(
        x2d, _row(mix_norm_g[0]), jnp.swapaxes(w_in.astype(F32), 1, 2),
        attn_sinks[0].astype(F32), _row(attn_out_norm_g[0]), w_out[0], l)

    x1, h2, w_up16, w_down16 = _ssd_outproj(
        x2d, xbc, z, dt, y_att,
        conv_w[0].astype(F32), _row(conv_b[0]),
        _row(dt_bias[0], DT_PAD), _row(A_log[0], DT_PAD),
        _row(jnp.repeat(D_skip[0], SSM_HEAD_DIM)), _row(ssm_norm_g[0]),
        w_out16, _row(mlp_norm_g[0]), w_up[0], w_down[0], l)
    out = _mlp(x1, h2, w_up16, w_down16, _row(final_norm_g))
    return out.reshape(b, l, d)
```

```python
import functools

import jax
import jax.numpy as jnp
from jax import lax
from jax.experimental import pallas as pl
from jax.experimental.pallas import tpu as pltpu

F32 = jnp.float32
BF16 = jnp.bfloat16

D_MODEL = 2048
SSM_D_INNER = 1024
SSM_HEAD_DIM = 64
SSM_N_HEADS = 16
SSM_N_GROUPS = 4
SSM_D_STATE = 128
SSM_CONV = 4
SSM_CHUNK = 128
SSM_GN = SSM_N_GROUPS * SSM_D_STATE
SSM_CONV_DIM = SSM_D_INNER + 2 * SSM_GN
SSM_GROUP_WIDTH = SSM_D_INNER // SSM_N_GROUPS
ATTN_WIDTH = 1024
ATTN_HEAD_DIM = 64
ATTN_N_HEADS = 16
ATTN_N_KV = 2
ATTN_KV_WIDTH = 2 * ATTN_N_KV * ATTN_HEAD_DIM
ATTN_BLOCK = 128
D_FF = 8192
EPS = 1e-5
LOG2E = 1.4426950408889634

LANES = 128
DT_PAD = LANES
SUBLANES = 8
MXU_WIDTH = 256
PROJ_ATTN_PATTERN = "abbbb"
VMEM_LIMIT = 56 * 1024 * 1024
VMEM_LIMIT_MLP = 62 * 1024 * 1024

SCAN_PROJ_PATTERN = "b" "abaababaabaa" "cbccbcbccbcc" "b"

TM_PROJ = 512
W_STAGE_ROWS = 208
TM_SSD = 256
TM_MLP = 512
TF_MLP = 2048


def _dot(a, b):
    return jnp.dot(a, b, preferred_element_type=F32)


def _split3(x):
    hi = x.astype(BF16)
    r = x - hi.astype(F32)
    mid = r.astype(BF16)
    lo = (r - mid.astype(F32)).astype(BF16)
    return hi, mid, lo


def _rms_scale(x):
    return lax.rsqrt(jnp.mean(x * x, axis=-1, keepdims=True) + EPS)


def _silu(x):
    hx = 0.5 * x
    return hx + hx * jnp.tanh(hx)


def _interleave(gens, pattern):
    live = {key: True for key in gens}
    while any(live.values()):
        for key in pattern:
            if live[key]:
                live[key] = next(gens[key], _DONE) is not _DONE


_DONE = object()


def _attn_block_stages(q_blk, kv_prev, kv_cur, first_block, sink_ref, emit):
    QB = ATTN_BLOCK
    D = ATTN_HEAD_DIM
    kv = jnp.concatenate([kv_prev, kv_cur], axis=0).astype(F32)
    k01 = kv[:, :2 * D]
    v01 = kv[:, 2 * D:]
    k_t = k01.T
    v01r = pltpu.roll(v01, D, axis=1)

    lane_kv = lax.broadcasted_iota(jnp.int32, (2 * QB, LANES), 1)
    first_kv = lane_kv < D
    ones_e = first_kv.astype(BF16)
    ones_o = 1 - ones_e
    zero_t = jnp.zeros((D, 2 * QB), BF16)

    i = lax.broadcasted_iota(jnp.int32, (QB, 2 * QB), 0)
    j = lax.broadcasted_iota(jnp.int32, (QB, 2 * QB), 1)
    valid = (j > i) & (j <= i + QB) & ((j >= QB) | jnp.logical_not(first_block))
    lane_q = lax.broadcasted_iota(jnp.int32, (QB, LANES), 1)
    first_q = lane_q < D

    per_kv = ATTN_N_HEADS // ATTN_N_KV // 2
    rhs, vb = [], []
    for g in range(ATTN_N_KV):
        kg_t = k_t[g * D:(g + 1) * D, :].astype(BF16)
        rhs.append(jnp.concatenate([jnp.concatenate([kg_t, zero_t], axis=1),
                                    jnp.concatenate([zero_t, kg_t], axis=1)], axis=0))
        if g == 0:
            v_e = jnp.where(first_kv, v01, 0.0)
            v_o = jnp.where(first_kv, 0.0, v01r)
        else:
            v_e = jnp.where(first_kv, v01r, 0.0)
            v_o = jnp.where(first_kv, 0.0, v01)
        vb.append(jnp.concatenate(
            [jnp.concatenate([v_e.astype(BF16), ones_e], axis=1),
             jnp.concatenate([v_o.astype(BF16), ones_o], axis=1)], axis=0))
    yield

    def scores(jp):
        s = _dot(q_blk(jp), rhs[jp // per_kv]) * (D ** -0.5)
        s_e = jnp.where(valid, s[:, :2 * QB], -jnp.inf)
        s_o = jnp.where(valid, s[:, 2 * QB:], -jnp.inf)
        sink_e = sink_ref[2 * jp]
        sink_o = sink_ref[2 * jp + 1]
        m_e = jnp.maximum(jnp.max(s_e, axis=-1, keepdims=True), sink_e)
        m_o = jnp.maximum(jnp.max(s_o, axis=-1, keepdims=True), sink_o)
        p = jnp.concatenate([jnp.exp(s_e - m_e), jnp.exp(s_o - m_o)], axis=1).astype(BF16)
        sink_term = jnp.where(first_q, jnp.exp(sink_e - m_e), jnp.exp(sink_o - m_o))
        return p, sink_term

    def values(jp, p, sink_term):
        o = _dot(p, vb[jp // per_kv])
        return o[:, :LANES] / (o[:, LANES:] + sink_term)

    n_pairs = ATTN_N_KV * per_kv
    outs = []
    pending = scores(0)
    yield
    for jp in range(1, n_pairs):
        nxt = scores(jp)
        yield
        outs.append(values(jp - 1, *pending))
        pending = nxt
        yield
    outs.append(values(n_pairs - 1, *pending))
    emit(jnp.concatenate(outs, axis=1))
    yield


def _inproj_attn_kernel(sink_ref, x_ref, g_ref, wt_hbm, ng_ref, wo_ref,
                        z_ref, xbc_ref, dt_ref, yatt_ref, wo16_ref,
                        wt_ref, stage, stage_sem,
                        cur_scr, prev_scr, kvtail_scr, *, tiles_per_seq):
    s = pl.program_id(0)
    n_tiles = pl.num_programs(0) - 1
    blocks = TM_PROJ // ATTN_BLOCK
    tail = slice(TM_PROJ - ATTN_BLOCK, TM_PROJ)

    wo16_ref[...] = wo_ref[...].astype(BF16)

    def stage_copy(r, slot):
        return pltpu.make_async_copy(wt_hbm.at[0, pl.ds(r * W_STAGE_ROWS, W_STAGE_ROWS), :],
                                     stage.at[slot], stage_sem.at[slot])

    def load_weights():
        n_chunks = wt_ref.shape[0] // W_STAGE_ROWS
        stage_copy(0, 0).start()
        for r in range(n_chunks):
            slot = r % 2
            if r + 1 < n_chunks:
                stage_copy(r + 1, 1 - slot).start()
            stage_copy(r, slot).wait()
            wt_ref[r * W_STAGE_ROWS:(r + 1) * W_STAGE_ROWS, :] = stage[slot].astype(BF16)

    def project():
        x = x_ref[...]
        h = (x * _rms_scale(x) * g_ref[...]).astype(BF16)
        yield

        def cols(lo, width):
            return lax.dot_general(h, wt_ref[lo:lo + width, :], (((1,), (1,)), ((), ())),
                                   preferred_element_type=F32)

        o_dt = SSM_D_INNER + SSM_CONV_DIM
        o_q = o_dt + SSM_N_HEADS
        for dst, base, width in ((z_ref, 0, SSM_D_INNER), (xbc_ref, SSM_D_INNER, SSM_CONV_DIM),
                                 (cur_scr, o_q, ATTN_WIDTH + ATTN_KV_WIDTH)):
            for lo in range(0, width, MXU_WIDTH):
                dst[:, lo:lo + MXU_WIDTH] = cols(base + lo, MXU_WIDTH).astype(BF16)
                yield
        lane = lax.broadcasted_iota(jnp.int32, (TM_PROJ, DT_PAD), 1)
        dt_ref[...] = jnp.where(lane < SSM_N_HEADS, cols(o_dt, DT_PAD), 0.0)
        yield

    def rotate_buffers():
        kvtail_scr[...] = prev_scr[tail, ATTN_WIDTH:]
        prev_scr[...] = cur_scr[...]

    def attend():
        seq_start = ((s - 1) % tiles_per_seq) == 0
        for r in range(blocks):
            lo = r * ATTN_BLOCK
            kv_cur = prev_scr[lo:lo + ATTN_BLOCK, ATTN_WIDTH:]
            if r == 0:
                kv_prev = kvtail_scr[...]
                first_block = seq_start
            else:
                kv_prev = prev_scr[lo - ATTN_BLOCK:lo, ATTN_WIDTH:]
                first_block = False
            q_blk = lambda jp, lo=lo: prev_scr[lo:lo + ATTN_BLOCK, jp * LANES:(jp + 1) * LANES]

            def emit(y, lo=lo):
                yatt_ref[lo:lo + ATTN_BLOCK, :] = (y * _rms_scale(y) * ng_ref[...]).astype(BF16)

            yield from _attn_block_stages(q_blk, kv_prev, kv_cur, first_block, sink_ref, emit)

    @pl.when(s == 0)
    def _():
        prev_scr[tail, ATTN_WIDTH:] = jnp.zeros((ATTN_BLOCK, ATTN_KV_WIDTH), BF16)
        load_weights()
        for _ in project():
            pass

    @pl.when((s > 0) & (s < n_tiles))
    def _():
        rotate_buffers()
        _interleave({"a": project(), "b": attend()}, PROJ_ATTN_PATTERN)

    @pl.when(s == n_tiles)
    def _():
        rotate_buffers()
        for _ in attend():
            pass


def _inproj_attn(x2d, g, w_in_t, sinks, attn_g, w_out, seq_len):
    t = x2d.shape[0]
    n_cols = w_in_t.shape[1]
    assert n_cols % W_STAGE_ROWS == 0 and w_in_t.shape[2] == D_MODEL
    tm = TM_PROJ
    n_tiles = t // tm
    cur = lambda width: pl.BlockSpec((tm, width), lambda i: (jnp.minimum(i, n_tiles - 1), 0))
    prev = lambda width: pl.BlockSpec((tm, width), lambda i: (jnp.maximum(i - 1, 0), 0))
    vec = lambda width: pl.BlockSpec((1, width), lambda i: (0, 0))
    wo_rows = pl.BlockSpec((D_MODEL // n_tiles, D_MODEL),
                           lambda i: (jnp.minimum(i, n_tiles - 1), 0))
    return pl.pallas_call(
        functools.partial(_inproj_attn_kernel, tiles_per_seq=seq_len // tm),
        grid=(n_tiles + 1,),
        in_specs=[
            pl.BlockSpec(memory_space=pltpu.SMEM),
            cur(D_MODEL),
            vec(D_MODEL),
            pl.BlockSpec(memory_space=pl.ANY),
            vec(ATTN_WIDTH),
            wo_rows,
        ],
        out_specs=[cur(SSM_D_INNER), cur(SSM_CONV_DIM), cur(DT_PAD), prev(ATTN_WIDTH), wo_rows],
        out_shape=[
            jax.ShapeDtypeStruct((t, SSM_D_INNER), BF16),
            jax.ShapeDtypeStruct((t, SSM_CONV_DIM), BF16),
            jax.ShapeDtypeStruct((t, DT_PAD), F32),
            jax.ShapeDtypeStruct((t, ATTN_WIDTH), BF16),
            jax.ShapeDtypeStruct((D_MODEL, D_MODEL), BF16),
        ],
        scratch_shapes=[pltpu.VMEM((n_cols, D_MODEL), BF16),
                        pltpu.VMEM((2, W_STAGE_ROWS, D_MODEL), F32),
                        pltpu.SemaphoreType.DMA((2,)),
                        pltpu.VMEM((tm, ATTN_WIDTH + ATTN_KV_WIDTH), BF16),
                        pltpu.VMEM((tm, ATTN_WIDTH + ATTN_KV_WIDTH), BF16),
                        pltpu.VMEM((ATTN_BLOCK, ATTN_KV_WIDTH), BF16)],
        compiler_params=pltpu.CompilerParams(
            dimension_semantics=("arbitrary",), vmem_limit_bytes=VMEM_LIMIT),
        name="inproj_attn",
    )(sinks, x2d, g, w_in_t, attn_g, w_out)


def _ssd_chunk_stages(xbc, z, dt_raw, cw_ref, cb_ref, dtb_ref, alog_ref, dskip_ref, ng_ref,
                      cbuf, state, emit):
    Q = SSM_CHUNK
    heads_per_group = SSM_N_HEADS // SSM_N_GROUPS
    dtr = dt_raw + dtb_ref[...]
    dt = jnp.maximum(dtr, 0.0) + jnp.log1p(jnp.exp(-jnp.abs(dtr)))
    a = dt * (-LOG2E * jnp.exp(alog_ref[...]))

    row = lax.broadcasted_iota(jnp.int32, (Q, Q), 0)
    col = lax.broadcasted_iota(jnp.int32, (Q, Q), 1)
    causal = row >= col
    tri = causal.astype(BF16)
    a_hi, a_mid, a_lo = _split3(a)
    a_cs = _dot(tri, a_hi) + _dot(tri, a_mid) + _dot(tri, a_lo)
    a_cs_t = a_cs.T
    yield

    cur = xbc.astype(F32)
    cbuf[SUBLANES:SUBLANES + Q, :] = cur
    acc = cb_ref[...] + cw_ref[SSM_CONV - 1:SSM_CONV, :] * cur
    for k in range(SSM_CONV - 1):
        lo = SUBLANES - (SSM_CONV - 1) + k
        acc = acc + cw_ref[k:k + 1, :] * cbuf[lo:lo + Q, :]
    cbuf[0:SUBLANES, :] = cur[Q - SUBLANES:Q, :]
    xc = _silu(acc)
    xs = xc[:, :SSM_D_INNER]
    yield

    er = lax.broadcasted_iota(jnp.int32, (LANES, SSM_D_INNER), 0)
    ec = lax.broadcasted_iota(jnp.int32, (LANES, SSM_D_INNER), 1)
    expand = ((ec // SSM_HEAD_DIM) == er).astype(BF16)
    stacked = jnp.concatenate([dt, a_cs], axis=0)
    s_hi, s_mid, _ = _split3(stacked)
    ex = _dot(s_hi, expand) + _dot(s_mid, expand)
    dt_x = ex[:Q]
    acs_x = ex[Q:]
    acs_last = acs_x[Q - 1:Q, :]
    yield

    xd = xs * dt_x
    xd16 = xd.astype(BF16)
    xdd16 = (xd * jnp.exp2(acs_last - acs_x)).astype(BF16)
    exp_acs = jnp.exp2(acs_x)
    chunk_decay = jnp.exp2(acs_last)

    lane = lax.broadcasted_iota(jnp.int32, (Q, LANES), 1)
    first_half = lane < SSM_HEAD_DIM
    zero16 = jnp.zeros((Q, LANES), BF16)

    b16, c16, cb = [], [], []
    for g in range(SSM_N_GROUPS):
        b16.append(xc[:, SSM_D_INNER + g * SSM_D_STATE:
                      SSM_D_INNER + (g + 1) * SSM_D_STATE].astype(BF16))
        c16.append(xc[:, SSM_D_INNER + SSM_GN + g * SSM_D_STATE:
                      SSM_D_INNER + SSM_GN + (g + 1) * SSM_D_STATE].astype(BF16))
        cb.append(lax.dot_general(c16[g], b16[g], (((1,), (1,)), ((), ())),
                                  preferred_element_type=F32))
    yield

    y_off = []
    for g in range(SSM_N_GROUPS):
        gl = g * SSM_GROUP_WIDTH
        st = state[:, gl:gl + SSM_GROUP_WIDTH]
        y_off.append(_dot(c16[g], st.astype(BF16)) * exp_acs[:, gl:gl + SSM_GROUP_WIDTH])
        upd = lax.dot_general(b16[g], xdd16[:, gl:gl + SSM_GROUP_WIDTH],
                              (((0,), (0,)), ((), ())), preferred_element_type=F32)
        state[:, gl:gl + SSM_GROUP_WIDTH] = chunk_decay[:, gl:gl + SSM_GROUP_WIDTH] * st + upd
    yield

    y_parts = []
    for g in range(SSM_N_GROUPS):
        ms = []
        for r in range(heads_per_group):
            h = g * heads_per_group + r
            seg = a_cs[:, h:h + 1] - a_cs_t[h:h + 1, :]
            lmat = jnp.exp2(jnp.where(causal, seg, -jnp.inf))
            ms.append((cb[g] * lmat).astype(BF16))
        gl = g * SSM_GROUP_WIDTH
        yd = []
        for pr in range(2):
            xp = xd16[:, gl + pr * LANES:gl + (pr + 1) * LANES]
            rhs = jnp.concatenate([jnp.where(first_half, xp, zero16),
                                   jnp.where(first_half, zero16, xp)], axis=0)
            lhs = jnp.concatenate([ms[2 * pr], ms[2 * pr + 1]], axis=1)
            yd.append(_dot(lhs, rhs))
        y_parts.append(jnp.concatenate(yd, axis=1) + y_off[g])
        if g % 2 == 1:
            yield

    y = jnp.concatenate(y_parts, axis=1) + xs * dskip_ref[...]
    y = y * _silu(z.astype(F32))
    outs = []
    for g in range(SSM_N_GROUPS):
        yg = y[:, g * SSM_GROUP_WIDTH:(g + 1) * SSM_GROUP_WIDTH]
        outs.append(yg * _rms_scale(yg))
    emit(jnp.concatenate(outs, axis=1) * ng_ref[...])
    yield


def _ssd_outproj_kernel(xbc_ref, z_ref, dt_ref, cw_ref, cb_ref, dtb_ref, alog_ref, dskip_ref,
                        ng_ref, x_ref, ya_ref, wo_ref, g2_ref, wu_ref, wd_ref,
                        x1_ref, h2_ref, wu16_ref, wd16_ref,
                        cbuf, state, ys_cur, ys_prev, *, tiles_per_seq):
    s = pl.program_id(0)
    n_tiles = pl.num_programs(0) - 1
    Q = SSM_CHUNK

    wu16_ref[...] = wu_ref[...].astype(BF16)
    wd16_ref[...] = wd_ref[...].astype(BF16)

    @pl.when((s % tiles_per_seq) == 0)
    def _():
        cbuf[0:SUBLANES, :] = jnp.zeros((SUBLANES, SSM_CONV_DIM), F32)
        state[...] = jnp.zeros_like(state)

    def chunk(c):
        lo = c * Q

        def emit(y):
            ys_cur[lo:lo + Q, :] = y.astype(BF16)

        return _ssd_chunk_stages(
            xbc_ref[lo:lo + Q, :], z_ref[lo:lo + Q, :], dt_ref[lo:lo + Q, :],
            cw_ref, cb_ref, dtb_ref, alog_ref, dskip_ref, ng_ref, cbuf, state, emit)

    def project():
        y = jnp.concatenate([ys_prev[...], ya_ref[...]], axis=1)
        yield
        for lo in range(0, D_MODEL, MXU_WIDTH):
            x1_ref[:, lo:lo + MXU_WIDTH] = (x_ref[:, lo:lo + MXU_WIDTH]
                                            + _dot(y, wo_ref[:, lo:lo + MXU_WIDTH]))
            yield
        x1 = x1_ref[...]
        h2_ref[...] = (x1 * _rms_scale(x1) * g2_ref[...]).astype(BF16)
        yield

    assert TM_SSD == 2 * Q, "the trace patterns below are written for two chunks per tile"

    @pl.when(s == 0)
    def _():
        _interleave({"a": chunk(0), "c": chunk(1)}, 8 * "a" + 8 * "c")

    @pl.when((s > 0) & (s < n_tiles))
    def _():
        ys_prev[...] = ys_cur[...]
        _interleave({"a": chunk(0), "c": chunk(1), "b": project()}, SCAN_PROJ_PATTERN)

    @pl.when(s == n_tiles)
    def _():
        ys_prev[...] = ys_cur[...]
        for _ in project():
            pass


def _ssd_outproj(x2d, xbc, z, dt, y_att, conv_w, conv_b, dt_bias, a_log, d_skip, norm_g,
                 w_out16, mlp_g, w_up, w_down, seq_len):
    t = x2d.shape[0]
    tm = TM_SSD
    n_tiles = t // tm
    cur = lambda width: pl.BlockSpec((tm, width), lambda i: (jnp.minimum(i, n_tiles - 1), 0))
    prev = lambda width: pl.BlockSpec((tm, width), lambda i: (jnp.maximum(i - 1, 0), 0))
    full = lambda r, width: pl.BlockSpec((r, width), lambda i: (0, 0))
    part = lambda r, width: pl.BlockSpec((r, width), lambda i: (jnp.minimum(i, n_tiles - 1), 0))
    ru, rd = D_MODEL // n_tiles, D_FF // n_tiles
    return pl.pallas_call(
        functools.partial(_ssd_outproj_kernel, tiles_per_seq=seq_len // tm),
        grid=(n_tiles + 1,),
        in_specs=[cur(SSM_CONV_DIM), cur(SSM_D_INNER), cur(DT_PAD),
                  full(SSM_CONV, SSM_CONV_DIM), full(1, SSM_CONV_DIM),
                  full(1, DT_PAD), full(1, DT_PAD),
                  full(1, SSM_D_INNER), full(1, SSM_D_INNER),
                  prev(D_MODEL), prev(ATTN_WIDTH),
                  pl.BlockSpec((D_MODEL, D_MODEL), lambda i: (0, 0), pipeline_mode=pl.Buffered(1)),
                  full(1, D_MODEL),
                  part(ru, D_FF), part(rd, D_MODEL)],
        out_specs=[prev(D_MODEL), prev(D_MODEL), part(ru, D_FF), part(rd, D_MODEL)],
        out_shape=[jax.ShapeDtypeStruct((t, D_MODEL), F32),
                   jax.ShapeDtypeStruct((t, D_MODEL), BF16),
                   jax.ShapeDtypeStruct((D_MODEL, D_FF), BF16),
                   jax.ShapeDtypeStruct((D_FF, D_MODEL), BF16)],
        scratch_shapes=[pltpu.VMEM((SUBLANES + SSM_CHUNK, SSM_CONV_DIM), F32),
                        pltpu.VMEM((SSM_D_STATE, SSM_D_INNER), F32),
                        pltpu.VMEM((tm, SSM_D_INNER), BF16),
                        pltpu.VMEM((tm, SSM_D_INNER), BF16)],
        compiler_params=pltpu.CompilerParams(
            dimension_semantics=("arbitrary",), vmem_limit_bytes=VMEM_LIMIT),
        name="ssd_outproj",
    )(xbc, z, dt, conv_w, conv_b, dt_bias, a_log, d_skip, norm_g,
      x2d, y_att, w_out16, mlp_g, w_up, w_down)


def _mlp_kernel(x1_ref, h2_ref, wu_ref, wd_ref, g_ref, o_ref):
    j = pl.program_id(1)

    @pl.when(j == 0)
    def _():
        o_ref[...] = x1_ref[...]

    u = jnp.maximum(_dot(h2_ref[...], wu_ref[...]), 0.0)
    o_ref[...] += _dot((u * u).astype(BF16), wd_ref[...])

    @pl.when(j == pl.num_programs(1) - 1)
    def _():
        x2 = o_ref[...]
        o_ref[...] = x2 * _rms_scale(x2) * g_ref[...]


def _mlp(x1, h2, w_up, w_down, g):
    t = x1.shape[0]
    tm, tf = TM_MLP, TF_MLP
    return pl.pallas_call(
        _mlp_kernel,
        grid=(t // tm, D_FF // tf),
        in_specs=[pl.BlockSpec((tm, D_MODEL), lambda i, j: (i, 0)),
                  pl.BlockSpec((tm, D_MODEL), lambda i, j: (i, 0)),
                  pl.BlockSpec((D_MODEL, tf), lambda i, j: (0, j)),
                  pl.BlockSpec((tf, D_MODEL), lambda i, j: (j, 0)),
                  pl.BlockSpec((1, D_MODEL), lambda i, j: (0, 0))],
        out_specs=pl.BlockSpec((tm, D_MODEL), lambda i, j: (i, 0)),
        out_shape=jax.ShapeDtypeStruct((t, D_MODEL), F32),
        compiler_params=pltpu.CompilerParams(
            dimension_semantics=("parallel", "arbitrary"), vmem_limit_bytes=VMEM_LIMIT_MLP),
        name="mlp",
    )(x1, h2, w_up, w_down, g)


def _row(v, width=None):
    v = v.astype(F32).reshape(1, -1)
    if width is not None and v.shape[1] < width:
        v = jnp.pad(v, ((0, 0), (0, width - v.shape[1])))
    return v


def kernel(x, mix_norm_g, w_in, conv_w, conv_b, dt_bias, A_log, D_skip, ssm_norm_g,
           attn_sinks, attn_out_norm_g, w_out, mlp_norm_g, w_up, w_down, final_norm_g):
    b, l, d = x.shape
    assert d == D_MODEL and l % SSM_CHUNK == 0 and l % TM_PROJ == 0
    assert w_in.shape[0] == 1, "one layer"
    x2d = x.reshape(b * l, d)

    z, xbc, dt, y_att, w_out16 = _inproj_attn(
        x2d, _row(mix_norm_g[0]), jnp.swapaxes(w_in.astype(F32), 1, 2),
        attn_sinks[0].astype(F32), _row(attn_out_norm_g[0]), w_out[0], l)

    x1, h2, w_up16, w_down16 = _ssd_outproj(
        x2d, xbc, z, dt, y_att,
        conv_w[0].astype(F32), _row(conv_b[0]),
        _row(dt_bias[0], DT_PAD), _row(A_log[0], DT_PAD),
        _row(jnp.repeat(D_skip[0], SSM_HEAD_DIM)), _row(ssm_norm_g[0]),
        w_out16, _row(mlp_norm_g[0]), w_up[0], w_down[0], l)
    out = _mlp(x1, h2, w_up16, w_down16, _row(final_norm_g))
    return out.reshape(b, l, d)
```

```python
import functools

import jax
import jax.numpy as jnp
from jax import lax
from jax.experimental import pallas as pl
from jax.experimental.pallas import tpu as pltpu

F32 = jnp.float32
BF16 = jnp.bfloat16

D_MODEL = 2048
SSM_D_INNER = 1024
SSM_HEAD_DIM = 64
SSM_N_HEADS = 16
SSM_N_GROUPS = 4
SSM_D_STATE = 128
SSM_CONV = 4
SSM_CHUNK = 128
SSM_GN = SSM_N_GROUPS * SSM_D_STATE
SSM_CONV_DIM = SSM_D_INNER + 2 * SSM_GN
SSM_GROUP_WIDTH = SSM_D_INNER // SSM_N_GROUPS
ATTN_WIDTH = 1024
ATTN_HEAD_DIM = 64
ATTN_N_HEADS = 16
ATTN_N_KV = 2
ATTN_KV_WIDTH = 2 * ATTN_N_KV * ATTN_HEAD_DIM
ATTN_BLOCK = 128
D_FF = 8192
EPS = 1e-5
LOG2E = 1.4426950408889634

LANES = 128
DT_PAD = LANES
SUBLANES = 8
MXU_WIDTH = 256
PROJ_ATTN_PATTERN = "abbbbc"
VMEM_LIMIT = 56 * 1024 * 1024
VMEM_LIMIT_BIG = 62 * 1024 * 1024

SCAN_PROJ_PATTERN = "b" "abaababaabaa" "cbccbcbccbcc" "wb"

TM_PROJ = 512
TM_SSD = 256
TM_MLP = 512
TF_MLP = 2048


def _dot(a, b):
    return jnp.dot(a, b, preferred_element_type=F32)


def _split3(x):
    hi = x.astype(BF16)
    r = x - hi.astype(F32)
    mid = r.astype(BF16)
    lo = (r - mid.astype(F32)).astype(BF16)
    return hi, mid, lo


def _rms_scale(x):
    return lax.rsqrt(jnp.mean(x * x, axis=-1, keepdims=True) + EPS)


def _silu(x):
    hx = 0.5 * x
    return hx + hx * jnp.tanh(hx)


def _interleave(gens, pattern):
    live = {key: True for key in gens}
    while any(live.values()):
        for key in pattern:
            if live[key]:
                live[key] = next(gens[key], _DONE) is not _DONE


_DONE = object()


def _attn_block_stages(q_blk, kv_prev, kv_cur, first_block, sink_ref, emit):
    QB = ATTN_BLOCK
    D = ATTN_HEAD_DIM
    kv = jnp.concatenate([kv_prev, kv_cur], axis=0).astype(F32)
    k01 = kv[:, :2 * D]
    v01 = kv[:, 2 * D:]
    k_t = k01.T
    v01r = pltpu.roll(v01, D, axis=1)

    lane_kv = lax.broadcasted_iota(jnp.int32, (2 * QB, LANES), 1)
    first_kv = lane_kv < D
    ones_e = first_kv.astype(BF16)
    ones_o = 1 - ones_e
    zero_t = jnp.zeros((D, 2 * QB), BF16)

    i = lax.broadcasted_iota(jnp.int32, (QB, 2 * QB), 0)
    j = lax.broadcasted_iota(jnp.int32, (QB, 2 * QB), 1)
    valid = (j > i) & (j <= i + QB) & ((j >= QB) | jnp.logical_not(first_block))
    lane_q = lax.broadcasted_iota(jnp.int32, (QB, LANES), 1)
    first_q = lane_q < D

    per_kv = ATTN_N_HEADS // ATTN_N_KV // 2
    rhs, vb = [], []
    for g in range(ATTN_N_KV):
        kg_t = k_t[g * D:(g + 1) * D, :].astype(BF16)
        rhs.append(jnp.concatenate([jnp.concatenate([kg_t, zero_t], axis=1),
                                    jnp.concatenate([zero_t, kg_t], axis=1)], axis=0))
        if g == 0:
            v_e = jnp.where(first_kv, v01, 0.0)
            v_o = jnp.where(first_kv, 0.0, v01r)
        else:
            v_e = jnp.where(first_kv, v01r, 0.0)
            v_o = jnp.where(first_kv, 0.0, v01)
        vb.append(jnp.concatenate(
            [jnp.concatenate([v_e.astype(BF16), ones_e], axis=1),
             jnp.concatenate([v_o.astype(BF16), ones_o], axis=1)], axis=0))
    yield

    def scores(jp):
        s = _dot(q_blk(jp), rhs[jp // per_kv]) * (D ** -0.5)
        s_e = jnp.where(valid, s[:, :2 * QB], -jnp.inf)
        s_o = jnp.where(valid, s[:, 2 * QB:], -jnp.inf)
        sink_e = sink_ref[2 * jp]
        sink_o = sink_ref[2 * jp + 1]
        m_e = jnp.maximum(jnp.max(s_e, axis=-1, keepdims=True), sink_e)
        m_o = jnp.maximum(jnp.max(s_o, axis=-1, keepdims=True), sink_o)
        p = jnp.concatenate([jnp.exp(s_e - m_e), jnp.exp(s_o - m_o)], axis=1).astype(BF16)
        sink_term = jnp.where(first_q, jnp.exp(sink_e - m_e), jnp.exp(sink_o - m_o))
        return p, sink_term

    def values(jp, p, sink_term):
        o = _dot(p, vb[jp // per_kv])
        return o[:, :LANES] / (o[:, LANES:] + sink_term)

    n_pairs = ATTN_N_KV * per_kv
    outs = []
    pending = scores(0)
    yield
    for jp in range(1, n_pairs):
        nxt = scores(jp)
        yield
        outs.append(values(jp - 1, *pending))
        pending = nxt
        yield
    outs.append(values(n_pairs - 1, *pending))
    emit(jnp.concatenate(outs, axis=1))
    yield


def _inproj_attn_kernel(sink_ref, x_ref, g_ref, wt_hbm, ng_ref, wo_ref, wu_ref,
                        z_ref, xbc_ref, dt_ref, yatt_ref, wo16_ref, wu16_ref,
                        wa_ref, wb_ref, wdt_ref, stage, stage_sem,
                        cur_scr, prev_scr, kvtail_scr, *, tiles_per_seq):
    s = pl.program_id(0)
    n_tiles = pl.num_programs(0) - 1
    blocks = TM_PROJ // ATTN_BLOCK
    tail = slice(TM_PROJ - ATTN_BLOCK, TM_PROJ)

    def cast_weights():
        wo16_ref[...] = wo_ref[...].astype(BF16)
        yield
        for lo in range(0, D_FF, D_MODEL):
            wu16_ref[:, lo:lo + D_MODEL] = wu_ref[:, lo:lo + D_MODEL].astype(BF16)
            yield

    o_dt = SSM_D_INNER + SSM_CONV_DIM
    o_q = o_dt + SSM_N_HEADS
    chunks = ([(r, wa_ref, r) for r in range(0, o_dt, LANES)] + [(o_dt, wdt_ref, 0)]
              + [(o_q + c, wb_ref, c) for c in range(0, ATTN_WIDTH + ATTN_KV_WIDTH, LANES)])

    def stage_copy(i, slot):
        return pltpu.make_async_copy(wt_hbm.at[0, pl.ds(chunks[i][0], LANES), :],
                                     stage.at[slot], stage_sem.at[slot])

    def load_weights():
        stage_copy(0, 0).start()
        for i, (_, dst, col) in enumerate(chunks):
            slot = i % 2
            if i + 1 < len(chunks):
                stage_copy(i + 1, 1 - slot).start()
            stage_copy(i, slot).wait()
            dst[:, col:col + LANES] = stage[slot].T.astype(BF16)

    def project():
        x = x_ref[...]
        h = (x * _rms_scale(x) * g_ref[...]).astype(BF16)
        yield
        for w_ref, dsts in ((wa_ref, ((z_ref, 0, SSM_D_INNER), (xbc_ref, SSM_D_INNER, SSM_CONV_DIM))),
                            (wb_ref, ((cur_scr, 0, ATTN_WIDTH + ATTN_KV_WIDTH),))):
            for dst, base, width in dsts:
                for lo in range(0, width, MXU_WIDTH):
                    dst[:, lo:lo + MXU_WIDTH] = _dot(
                        h, w_ref[:, base + lo:base + lo + MXU_WIDTH]).astype(BF16)
                    yield
        lane = lax.broadcasted_iota(jnp.int32, (TM_PROJ, DT_PAD), 1)
        dt_ref[...] = jnp.where(lane < SSM_N_HEADS, _dot(h, wdt_ref[...]), 0.0)
        yield

    def rotate_buffers():
        kvtail_scr[...] = prev_scr[tail, ATTN_WIDTH:]
        prev_scr[...] = cur_scr[...]

    def attend():
        seq_start = ((s - 1) % tiles_per_seq) == 0
        for r in range(blocks):
            lo = r * ATTN_BLOCK
            kv_cur = prev_scr[lo:lo + ATTN_BLOCK, ATTN_WIDTH:]
            if r == 0:
                kv_prev = kvtail_scr[...]
                first_block = seq_start
            else:
                kv_prev = prev_scr[lo - ATTN_BLOCK:lo, ATTN_WIDTH:]
                first_block = False
            q_blk = lambda jp, lo=lo: prev_scr[lo:lo + ATTN_BLOCK, jp * LANES:(jp + 1) * LANES]

            def emit(y, lo=lo):
                yatt_ref[lo:lo + ATTN_BLOCK, :] = (y * _rms_scale(y) * ng_ref[...]).astype(BF16)

            yield from _attn_block_stages(q_blk, kv_prev, kv_cur, first_block, sink_ref, emit)

    @pl.when(s == 0)
    def _():
        prev_scr[tail, ATTN_WIDTH:] = jnp.zeros((ATTN_BLOCK, ATTN_KV_WIDTH), BF16)
        load_weights()
        _interleave({"a": project(), "c": cast_weights()}, "ac")

    @pl.when((s > 0) & (s < n_tiles))
    def _():
        rotate_buffers()
        _interleave({"a": project(), "b": attend(), "c": cast_weights()}, PROJ_ATTN_PATTERN)

    @pl.when(s == n_tiles)
    def _():
        rotate_buffers()
        for _ in attend():
            pass


def _inproj_attn(x2d, g, w_in_t, sinks, attn_g, w_out, w_up, seq_len):
    t = x2d.shape[0]
    n_cols = SSM_D_INNER + SSM_CONV_DIM + SSM_N_HEADS + ATTN_WIDTH + ATTN_KV_WIDTH
    assert w_in_t.shape == (1, n_cols, D_MODEL)
    tm = TM_PROJ
    n_tiles = t // tm
    cur = lambda width: pl.BlockSpec((tm, width), lambda i: (jnp.minimum(i, n_tiles - 1), 0))
    prev = lambda width: pl.BlockSpec((tm, width), lambda i: (jnp.maximum(i - 1, 0), 0))
    vec = lambda width: pl.BlockSpec((1, width), lambda i: (0, 0))
    w_rows = lambda width: pl.BlockSpec((D_MODEL // n_tiles, width),
                                        lambda i: (jnp.minimum(i, n_tiles - 1), 0))
    return pl.pallas_call(
        functools.partial(_inproj_attn_kernel, tiles_per_seq=seq_len // tm),
        grid=(n_tiles + 1,),
        in_specs=[
            pl.BlockSpec(memory_space=pltpu.SMEM),
            cur(D_MODEL),
            vec(D_MODEL),
            pl.BlockSpec(memory_space=pl.ANY),
            vec(ATTN_WIDTH),
            w_rows(D_MODEL),
            w_rows(D_FF),
        ],
        out_specs=[cur(SSM_D_INNER), cur(SSM_CONV_DIM), cur(DT_PAD), prev(ATTN_WIDTH),
                   w_rows(D_MODEL), w_rows(D_FF)],
        out_shape=[
            jax.ShapeDtypeStruct((t, SSM_D_INNER), BF16),
            jax.ShapeDtypeStruct((t, SSM_CONV_DIM), BF16),
            jax.ShapeDtypeStruct((t, DT_PAD), F32),
            jax.ShapeDtypeStruct((t, ATTN_WIDTH), BF16),
            jax.ShapeDtypeStruct((D_MODEL, D_MODEL), BF16),
            jax.ShapeDtypeStruct((D_MODEL, D_FF), BF16),
        ],
        scratch_shapes=[pltpu.VMEM((D_MODEL, SSM_D_INNER + SSM_CONV_DIM), BF16),
                        pltpu.VMEM((D_MODEL, ATTN_WIDTH + ATTN_KV_WIDTH), BF16),
                        pltpu.VMEM((D_MODEL, DT_PAD), BF16),
                        pltpu.VMEM((2, LANES, D_MODEL), F32),
                        pltpu.SemaphoreType.DMA((2,)),
                        pltpu.VMEM((tm, ATTN_WIDTH + ATTN_KV_WIDTH), BF16),
                        pltpu.VMEM((tm, ATTN_WIDTH + ATTN_KV_WIDTH), BF16),
                        pltpu.VMEM((ATTN_BLOCK, ATTN_KV_WIDTH), BF16)],
        compiler_params=pltpu.CompilerParams(
            dimension_semantics=("arbitrary",), vmem_limit_bytes=VMEM_LIMIT_BIG),
        name="inproj_attn",
    )(sinks, x2d, g, w_in_t, attn_g, w_out, w_up)


def _ssd_chunk_stages(xbc, z, dt_raw, cw_ref, cb_ref, dtb_ref, alog_ref, dskip_ref, ng_ref,
                      cbuf, state, emit):
    Q = SSM_CHUNK
    heads_per_group = SSM_N_HEADS // SSM_N_GROUPS
    dtr = dt_raw + dtb_ref[...]
    dt = jnp.maximum(dtr, 0.0) + jnp.log1p(jnp.exp(-jnp.abs(dtr)))
    a = dt * (-LOG2E * jnp.exp(alog_ref[...]))

    row = lax.broadcasted_iota(jnp.int32, (Q, Q), 0)
    col = lax.broadcasted_iota(jnp.int32, (Q, Q), 1)
    causal = row >= col
    tri = causal.astype(BF16)
    a_hi, a_mid, a_lo = _split3(a)
    a_cs = _dot(tri, a_hi) + _dot(tri, a_mid) + _dot(tri, a_lo)
    a_cs_t = a_cs.T
    yield

    cur = xbc.astype(F32)
    cbuf[SUBLANES:SUBLANES + Q, :] = cur
    acc = cb_ref[...] + cw_ref[SSM_CONV - 1:SSM_CONV, :] * cur
    for k in range(SSM_CONV - 1):
        lo = SUBLANES - (SSM_CONV - 1) + k
        acc = acc + cw_ref[k:k + 1, :] * cbuf[lo:lo + Q, :]
    cbuf[0:SUBLANES, :] = cur[Q - SUBLANES:Q, :]
    xc = _silu(acc)
    xs = xc[:, :SSM_D_INNER]
    yield

    er = lax.broadcasted_iota(jnp.int32, (LANES, SSM_D_INNER), 0)
    ec = lax.broadcasted_iota(jnp.int32, (LANES, SSM_D_INNER), 1)
    expand = ((ec // SSM_HEAD_DIM) == er).astype(BF16)
    stacked = jnp.concatenate([dt, a_cs], axis=0)
    s_hi, s_mid, _ = _split3(stacked)
    ex = _dot(s_hi, expand) + _dot(s_mid, expand)
    dt_x = ex[:Q]
    acs_x = ex[Q:]
    acs_last = acs_x[Q - 1:Q, :]
    yield

    xd = xs * dt_x
    xd16 = xd.astype(BF16)
    xdd16 = (xd * jnp.exp2(acs_last - acs_x)).astype(BF16)
    exp_acs = jnp.exp2(acs_x)
    chunk_decay = jnp.exp2(acs_last)

    lane = lax.broadcasted_iota(jnp.int32, (Q, LANES), 1)
    first_half = lane < SSM_HEAD_DIM
    zero16 = jnp.zeros((Q, LANES), BF16)

    b16, c16, cb = [], [], []
    for g in range(SSM_N_GROUPS):
        b16.append(xc[:, SSM_D_INNER + g * SSM_D_STATE:
                      SSM_D_INNER + (g + 1) * SSM_D_STATE].astype(BF16))
        c16.append(xc[:, SSM_D_INNER + SSM_GN + g * SSM_D_STATE:
                      SSM_D_INNER + SSM_GN + (g + 1) * SSM_D_STATE].astype(BF16))
        cb.append(lax.dot_general(c16[g], b16[g], (((1,), (1,)), ((), ())),
                                  preferred_element_type=F32))
    yield

    y_off = []
    for g in range(SSM_N_GROUPS):
        gl = g * SSM_GROUP_WIDTH
        st = state[:, gl:gl + SSM_GROUP_WIDTH]
        y_off.append(_dot(c16[g], st.astype(BF16)) * exp_acs[:, gl:gl + SSM_GROUP_WIDTH])
        upd = lax.dot_general(b16[g], xdd16[:, gl:gl + SSM_GROUP_WIDTH],
                              (((0,), (0,)), ((), ())), preferred_element_type=F32)
        state[:, gl:gl + SSM_GROUP_WIDTH] = chunk_decay[:, gl:gl + SSM_GROUP_WIDTH] * st + upd
    yield

    y_parts = []
    for g in range(SSM_N_GROUPS):
        ms = []
        for r in range(heads_per_group):
            h = g * heads_per_group + r
            seg = a_cs[:, h:h + 1] - a_cs_t[h:h + 1, :]
            lmat = jnp.exp2(jnp.where(causal, seg, -jnp.inf))
            ms.append((cb[g] * lmat).astype(BF16))
        gl = g * SSM_GROUP_WIDTH
        yd = []
        for pr in range(2):
            xp = xd16[:, gl + pr * LANES:gl + (pr + 1) * LANES]
            rhs = jnp.concatenate([jnp.where(first_half, xp, zero16),
                                   jnp.where(first_half, zero16, xp)], axis=0)
            lhs = jnp.concatenate([ms[2 * pr], ms[2 * pr + 1]], axis=1)
            yd.append(_dot(lhs, rhs))
        y_parts.append(jnp.concatenate(yd, axis=1) + y_off[g])
        if g % 2 == 1:
            yield

    y = jnp.concatenate(y_parts, axis=1) + xs * dskip_ref[...]
    y = y * _silu(z.astype(F32))
    outs = []
    for g in range(SSM_N_GROUPS):
        yg = y[:, g * SSM_GROUP_WIDTH:(g + 1) * SSM_GROUP_WIDTH]
        outs.append(yg * _rms_scale(yg))
    emit(jnp.concatenate(outs, axis=1) * ng_ref[...])
    yield


def _ssd_outproj_kernel(xbc_ref, z_ref, dt_ref, cw_ref, cb_ref, dtb_ref, alog_ref, dskip_ref,
                        ng_ref, x_ref, ya_ref, wo_ref, g2_ref, wd_ref,
                        x1_ref, h2_ref, wd16_ref,
                        cbuf, state, ys_cur, ys_prev, *, tiles_per_seq):
    s = pl.program_id(0)
    n_tiles = pl.num_programs(0) - 1
    Q = SSM_CHUNK

    def cast_weights():
        wd16_ref[...] = wd_ref[...].astype(BF16)
        yield

    @pl.when((s % tiles_per_seq) == 0)
    def _():
        cbuf[0:SUBLANES, :] = jnp.zeros((SUBLANES, SSM_CONV_DIM), F32)
        state[...] = jnp.zeros_like(state)

    def chunk(c):
        lo = c * Q

        def emit(y):
            ys_cur[lo:lo + Q, :] = y.astype(BF16)

        return _ssd_chunk_stages(
            xbc_ref[lo:lo + Q, :], z_ref[lo:lo + Q, :], dt_ref[lo:lo + Q, :],
            cw_ref, cb_ref, dtb_ref, alog_ref, dskip_ref, ng_ref, cbuf, state, emit)

    def project():
        y = jnp.concatenate([ys_prev[...], ya_ref[...]], axis=1)
        yield
        for lo in range(0, D_MODEL, MXU_WIDTH):
            x1_ref[:, lo:lo + MXU_WIDTH] = (x_ref[:, lo:lo + MXU_WIDTH]
                                            + _dot(y, wo_ref[:, lo:lo + MXU_WIDTH]))
            yield
        x1 = x1_ref[...]
        h2_ref[...] = (x1 * _rms_scale(x1) * g2_ref[...]).astype(BF16)
        yield

    assert TM_SSD == 2 * Q, "the trace patterns below are written for two chunks per tile"

    @pl.when(s == 0)
    def _():
        _interleave({"a": chunk(0), "c": chunk(1), "w": cast_weights()}, 8 * "a" + 8 * "c" + "w")

    @pl.when((s > 0) & (s < n_tiles))
    def _():
        ys_prev[...] = ys_cur[...]
        _interleave({"a": chunk(0), "c": chunk(1), "b": project(), "w": cast_weights()},
                    SCAN_PROJ_PATTERN)

    @pl.when(s == n_tiles)
    def _():
        ys_prev[...] = ys_cur[...]
        for _ in project():
            pass


def _ssd_outproj(x2d, xbc, z, dt, y_att, conv_w, conv_b, dt_bias, a_log, d_skip, norm_g,
                 w_out16, mlp_g, w_down, seq_len):
    t = x2d.shape[0]
    tm = TM_SSD
    n_tiles = t // tm
    cur = lambda width: pl.BlockSpec((tm, width), lambda i: (jnp.minimum(i, n_tiles - 1), 0))
    prev = lambda width: pl.BlockSpec((tm, width), lambda i: (jnp.maximum(i - 1, 0), 0))
    full = lambda r, width: pl.BlockSpec((r, width), lambda i: (0, 0))
    part = lambda r, width: pl.BlockSpec((r, width), lambda i: (jnp.minimum(i, n_tiles - 1), 0))
    rd = D_FF // n_tiles
    return pl.pallas_call(
        functools.partial(_ssd_outproj_kernel, tiles_per_seq=seq_len // tm),
        grid=(n_tiles + 1,),
        in_specs=[cur(SSM_CONV_DIM), cur(SSM_D_INNER), cur(DT_PAD),
                  full(SSM_CONV, SSM_CONV_DIM), full(1, SSM_CONV_DIM),
                  full(1, DT_PAD), full(1, DT_PAD),
                  full(1, SSM_D_INNER), full(1, SSM_D_INNER),
                  prev(D_MODEL), prev(ATTN_WIDTH),
                  pl.BlockSpec((D_MODEL, D_MODEL), lambda i: (0, 0), pipeline_mode=pl.Buffered(1)),
                  full(1, D_MODEL),
                  part(rd, D_MODEL)],
        out_specs=[prev(D_MODEL), prev(D_MODEL), part(rd, D_MODEL)],
        out_shape=[jax.ShapeDtypeStruct((t, D_MODEL), F32),
                   jax.ShapeDtypeStruct((t, D_MODEL), BF16),
                   jax.ShapeDtypeStruct((D_FF, D_MODEL), BF16)],
        scratch_shapes=[pltpu.VMEM((SUBLANES + SSM_CHUNK, SSM_CONV_DIM), F32),
                        pltpu.VMEM((SSM_D_STATE, SSM_D_INNER), F32),
                        pltpu.VMEM((tm, SSM_D_INNER), BF16),
                        pltpu.VMEM((tm, SSM_D_INNER), BF16)],
        compiler_params=pltpu.CompilerParams(
            dimension_semantics=("arbitrary",), vmem_limit_bytes=VMEM_LIMIT),
        name="ssd_outproj",
    )(xbc, z, dt, conv_w, conv_b, dt_bias, a_log, d_skip, norm_g,
      x2d, y_att, w_out16, mlp_g, w_down)


def _mlp_kernel(x1_ref, h2_ref, wu_ref, wd_ref, g_ref, o_ref):
    j = pl.program_id(1)

    @pl.when(j == 0)
    def _():
        o_ref[...] = x1_ref[...]

    u = jnp.maximum(_dot(h2_ref[...], wu_ref[...]), 0.0)
    o_ref[...] += _dot((u * u).astype(BF16), wd_ref[...])

    @pl.when(j == pl.num_programs(1) - 1)
    def _():
        x2 = o_ref[...]
        o_ref[...] = x2 * _rms_scale(x2) * g_ref[...]


def _mlp(x1, h2, w_up, w_down, g):
    t = x1.shape[0]
    tm, tf = TM_MLP, TF_MLP
    return pl.pallas_call(
        _mlp_kernel,
        grid=(t // tm, D_FF // tf),
        in_specs=[pl.BlockSpec((tm, D_MODEL), lambda i, j: (i, 0)),
                  pl.BlockSpec((tm, D_MODEL), lambda i, j: (i, 0)),
                  pl.BlockSpec((D_MODEL, tf), lambda i, j: (0, j)),
                  pl.BlockSpec((tf, D_MODEL), lambda i, j: (j, 0)),
                  pl.BlockSpec((1, D_MODEL), lambda i, j: (0, 0))],
        out_specs=pl.BlockSpec((tm, D_MODEL), lambda i, j: (i, 0)),
        out_shape=jax.ShapeDtypeStruct((t, D_MODEL), F32),
        compiler_params=pltpu.CompilerParams(
            dimension_semantics=("parallel", "arbitrary"), vmem_limit_bytes=VMEM_LIMIT_BIG),
        name="mlp",
    )(x1, h2, w_up, w_down, g)


def _row(v, width=None):
    v = v.astype(F32).reshape(1, -1)
    if width is not None and v.shape[1] < width:
        v = jnp.pad(v, ((0, 0), (0, width - v.shape[1])))
    return v


def kernel(x, mix_norm_g, w_in, conv_w, conv_b, dt_bias, A_log, D_skip, ssm_norm_g,
           attn_sinks, attn_out_norm_g, w_out, mlp_norm_g, w_up, w_down, final_norm_g):
    b, l, d = x.shape
    assert d == D_MODEL and l % SSM_CHUNK == 0 and l % TM_PROJ == 0
    assert w_in.shape[0] == 1, "one layer"
    x2d = x.reshape(b * l, d)

    z, xbc, dt, y_att, w_out16, w_up16 = _inproj_attn(
        x2d, _row(mix_norm_g[0]), jnp.swapaxes(w_in.astype(F32), 1, 2),
        attn_sinks[0].astype(F32), _row(attn_out_norm_g[0]), w_out[0], w_up[0], l)

    x1, h2, w_down16 = _ssd_outproj(
        x2d, xbc, z, dt, y_att,
        conv_w[0].astype(F32), _row(conv_b[0]),
        _row(dt_bias[0], DT_PAD), _row(A_log[0], DT_PAD),
        _row(jnp.repeat(D_skip[0], SSM_HEAD_DIM)), _row(ssm_norm_g[0]),
        w_out16, _row(mlp_norm_g[0]), w_down[0], l)
    out = _mlp(x1, h2, w_up16, w_down16, _row(final_norm_g))
    return out.reshape(b, l, d)
```

```python
import functools

import jax
import jax.numpy as jnp
from jax import lax
from jax.experimental import pallas as pl
from jax.experimental.pallas import tpu as pltpu

F32 = jnp.float32
BF16 = jnp.bfloat16

D_MODEL = 2048
SSM_D_INNER = 1024
SSM_HEAD_DIM = 64
SSM_N_HEADS = 16
SSM_N_GROUPS = 4
SSM_D_STATE = 128
SSM_CONV = 4
SSM_CHUNK = 128
SSM_GN = SSM_N_GROUPS * SSM_D_STATE
SSM_CONV_DIM = SSM_D_INNER + 2 * SSM_GN
SSM_GROUP_WIDTH = SSM_D_INNER // SSM_N_GROUPS
ATTN_WIDTH = 1024
ATTN_HEAD_DIM = 64
ATTN_N_HEADS = 16
ATTN_N_KV = 2
ATTN_KV_WIDTH = 2 * ATTN_N_KV * ATTN_HEAD_DIM
ATTN_BLOCK = 128
D_FF = 8192
EPS = 1e-5
LOG2E = 1.4426950408889634

LANES = 128
DT_PAD = LANES
SUBLANES = 8
MXU_WIDTH = 256
PROJ_ATTN_PATTERN = "abbbc"
VMEM_LIMIT = 56 * 1024 * 1024
VMEM_LIMIT_BIG = 62 * 1024 * 1024

SCAN_PROJ_PATTERN = "b" "abaababaabaa" "cbccbcbccbcc" "wb"

TM_PROJ = 512
W_STAGE_ROWS = 208
TM_SSD = 256
TM_MLP = 512
TF_MLP = 2048


def _dot(a, b):
    return jnp.dot(a, b, preferred_element_type=F32)


def _split3(x):
    hi = x.astype(BF16)
    r = x - hi.astype(F32)
    mid = r.astype(BF16)
    lo = (r - mid.astype(F32)).astype(BF16)
    return hi, mid, lo


def _rms_scale(x):
    return lax.rsqrt(jnp.mean(x * x, axis=-1, keepdims=True) + EPS)


def _silu(x):
    hx = 0.5 * x
    return hx + hx * jnp.tanh(hx)


def _interleave(gens, pattern):
    live = {key: True for key in gens}
    while any(live.values()):
        for key in pattern:
            if live[key]:
                live[key] = next(gens[key], _DONE) is not _DONE


_DONE = object()


def _attn_block_stages(q_blk, kv_prev, kv_cur, first_block, sink_ref, emit):
    QB = ATTN_BLOCK
    D = ATTN_HEAD_DIM
    kv = jnp.concatenate([kv_prev, kv_cur], axis=0).astype(F32)
    k01 = kv[:, :2 * D]
    v01 = kv[:, 2 * D:]
    k_t = k01.T
    v01r = pltpu.roll(v01, D, axis=1)

    lane_kv = lax.broadcasted_iota(jnp.int32, (2 * QB, LANES), 1)
    first_kv = lane_kv < D
    ones_e = first_kv.astype(BF16)
    ones_o = 1 - ones_e
    zero_t = jnp.zeros((D, 2 * QB), BF16)

    i = lax.broadcasted_iota(jnp.int32, (QB, 2 * QB), 0)
    j = lax.broadcasted_iota(jnp.int32, (QB, 2 * QB), 1)
    valid = (j > i) & (j <= i + QB) & ((j >= QB) | jnp.logical_not(first_block))
    lane_q = lax.broadcasted_iota(jnp.int32, (QB, LANES), 1)
    first_q = lane_q < D

    per_kv = ATTN_N_HEADS // ATTN_N_KV // 2
    rhs, vb = [], []
    for g in range(ATTN_N_KV):
        kg_t = k_t[g * D:(g + 1) * D, :].astype(BF16)
        rhs.append(jnp.concatenate([jnp.concatenate([kg_t, zero_t], axis=1),
                                    jnp.concatenate([zero_t, kg_t], axis=1)], axis=0))
        if g == 0:
            v_e = jnp.where(first_kv, v01, 0.0)
            v_o = jnp.where(first_kv, 0.0, v01r)
        else:
            v_e = jnp.where(first_kv, v01r, 0.0)
            v_o = jnp.where(first_kv, 0.0, v01)
        vb.append(jnp.concatenate(
            [jnp.concatenate([v_e.astype(BF16), ones_e], axis=1),
             jnp.concatenate([v_o.astype(BF16), ones_o], axis=1)], axis=0))
    yield

    def scores(jp):
        s = _dot(q_blk(jp), rhs[jp // per_kv]) * (D ** -0.5)
        s_e = jnp.where(valid, s[:, :2 * QB], -jnp.inf)
        s_o = jnp.where(valid, s[:, 2 * QB:], -jnp.inf)
        sink_e = sink_ref[2 * jp]
        sink_o = sink_ref[2 * jp + 1]
        m_e = jnp.maximum(jnp.max(s_e, axis=-1, keepdims=True), sink_e)
        m_o = jnp.maximum(jnp.max(s_o, axis=-1, keepdims=True), sink_o)
        p = jnp.concatenate([jnp.exp(s_e - m_e), jnp.exp(s_o - m_o)], axis=1).astype(BF16)
        sink_term = jnp.where(first_q, jnp.exp(sink_e - m_e), jnp.exp(sink_o - m_o))
        return p, sink_term

    def values(jp, p, sink_term):
        o = _dot(p, vb[jp // per_kv])
        return o[:, :LANES] / (o[:, LANES:] + sink_term)

    n_pairs = ATTN_N_KV * per_kv
    outs = []
    pending = scores(0)
    yield
    for jp in range(1, n_pairs):
        nxt = scores(jp)
        yield
        outs.append(values(jp - 1, *pending))
        pending = nxt
        yield
    outs.append(values(n_pairs - 1, *pending))
    emit(jnp.concatenate(outs, axis=1))
    yield


def _inproj_attn_kernel(sink_ref, x_ref, g_ref, wt_hbm, ng_ref, cw_ref, cb_ref, wo_ref, wu_ref,
                        z_ref, xbc_ref, dt_ref, yatt_ref, wo16_ref, wu16_ref,
                        wt_ref, stage, stage_sem, cbuf,
                        cur_scr, prev_scr, kvtail_scr, *, tiles_per_seq):
    s = pl.program_id(0)
    n_tiles = pl.num_programs(0) - 1
    blocks = TM_PROJ // ATTN_BLOCK
    tail = slice(TM_PROJ - ATTN_BLOCK, TM_PROJ)

    def cast_weights():
        wo16_ref[...] = wo_ref[...].astype(BF16)
        yield
        for lo in range(0, D_FF, D_MODEL):
            wu16_ref[:, lo:lo + D_MODEL] = wu_ref[:, lo:lo + D_MODEL].astype(BF16)
            yield

    def stage_copy(r, slot):
        return pltpu.make_async_copy(wt_hbm.at[0, pl.ds(r * W_STAGE_ROWS, W_STAGE_ROWS), :],
                                     stage.at[slot], stage_sem.at[slot])

    def load_weights():
        n_chunks = wt_ref.shape[0] // W_STAGE_ROWS
        stage_copy(0, 0).start()
        for r in range(n_chunks):
            slot = r % 2
            if r + 1 < n_chunks:
                stage_copy(r + 1, 1 - slot).start()
            stage_copy(r, slot).wait()
            wt_ref[r * W_STAGE_ROWS:(r + 1) * W_STAGE_ROWS, :] = stage[slot].astype(BF16)

    def project():
        x = x_ref[...]
        h = (x * _rms_scale(x) * g_ref[...]).astype(BF16)
        yield

        def cols(lo, width):
            return lax.dot_general(h, wt_ref[lo:lo + width, :], (((1,), (1,)), ((), ())),
                                   preferred_element_type=F32)

        o_dt = SSM_D_INNER + SSM_CONV_DIM
        o_q = o_dt + SSM_N_HEADS
        for lo in range(0, SSM_CONV_DIM, MXU_WIDTH):
            xbc_ref[:, lo:lo + MXU_WIDTH] = cols(SSM_D_INNER + lo, MXU_WIDTH).astype(BF16)
            yield
        rest = ([(z_ref, lo, lo) for lo in range(0, SSM_D_INNER, MXU_WIDTH)]
                + [(cur_scr, lo, o_q + lo)
                   for lo in range(0, ATTN_WIDTH + ATTN_KV_WIDTH, MXU_WIDTH)])
        for c in range(TM_PROJ // SSM_CHUNK):
            rows = slice(c * SSM_CHUNK, (c + 1) * SSM_CHUNK)
            seq_start = ((s % tiles_per_seq) == 0) if c == 0 else False
            xbc_ref[rows, :] = _conv_silu(xbc_ref[rows, :], seq_start, cw_ref, cb_ref, cbuf)
            yield
            for dst, lo, src in (rest.pop(0), rest.pop(0)):
                dst[:, lo:lo + MXU_WIDTH] = cols(src, MXU_WIDTH).astype(BF16)
                yield
        for dst, lo, src in rest:
            dst[:, lo:lo + MXU_WIDTH] = cols(src, MXU_WIDTH).astype(BF16)
            yield
        lane = lax.broadcasted_iota(jnp.int32, (TM_PROJ, DT_PAD), 1)
        dt_ref[...] = jnp.where(lane < SSM_N_HEADS, cols(o_dt, DT_PAD), 0.0)
        yield

    def rotate_buffers():
        kvtail_scr[...] = prev_scr[tail, ATTN_WIDTH:]
        prev_scr[...] = cur_scr[...]

    def attend():
        seq_start = ((s - 1) % tiles_per_seq) == 0
        for r in range(blocks):
            lo = r * ATTN_BLOCK
            kv_cur = prev_scr[lo:lo + ATTN_BLOCK, ATTN_WIDTH:]
            if r == 0:
                kv_prev = kvtail_scr[...]
                first_block = seq_start
            else:
                kv_prev = prev_scr[lo - ATTN_BLOCK:lo, ATTN_WIDTH:]
                first_block = False
            q_blk = lambda jp, lo=lo: prev_scr[lo:lo + ATTN_BLOCK, jp * LANES:(jp + 1) * LANES]

            def emit(y, lo=lo):
                yatt_ref[lo:lo + ATTN_BLOCK, :] = (y * _rms_scale(y) * ng_ref[...]).astype(BF16)

            yield from _attn_block_stages(q_blk, kv_prev, kv_cur, first_block, sink_ref, emit)

    @pl.when(s == 0)
    def _():
        prev_scr[tail, ATTN_WIDTH:] = jnp.zeros((ATTN_BLOCK, ATTN_KV_WIDTH), BF16)
        cbuf[0:SUBLANES, :] = jnp.zeros((SUBLANES, SSM_CONV_DIM), F32)
        load_weights()
        _interleave({"a": project(), "c": cast_weights()}, "ac")

    @pl.when((s > 0) & (s < n_tiles))
    def _():
        rotate_buffers()
        _interleave({"a": project(), "b": attend(), "c": cast_weights()}, PROJ_ATTN_PATTERN)

    @pl.when(s == n_tiles)
    def _():
        rotate_buffers()
        for _ in attend():
            pass


def _inproj_attn(x2d, g, w_in_t, sinks, attn_g, conv_w, conv_b, w_out, w_up, seq_len):
    t = x2d.shape[0]
    n_cols = w_in_t.shape[1]
    assert n_cols % W_STAGE_ROWS == 0 and w_in_t.shape[2] == D_MODEL
    tm = TM_PROJ
    n_tiles = t // tm
    cur = lambda width: pl.BlockSpec((tm, width), lambda i: (jnp.minimum(i, n_tiles - 1), 0))
    prev = lambda width: pl.BlockSpec((tm, width), lambda i: (jnp.maximum(i - 1, 0), 0))
    vec = lambda width: pl.BlockSpec((1, width), lambda i: (0, 0))
    w_rows = lambda width: pl.BlockSpec((D_MODEL // n_tiles, width),
                                        lambda i: (jnp.minimum(i, n_tiles - 1), 0))
    return pl.pallas_call(
        functools.partial(_inproj_attn_kernel, tiles_per_seq=seq_len // tm),
        grid=(n_tiles + 1,),
        in_specs=[
            pl.BlockSpec(memory_space=pltpu.SMEM),
            cur(D_MODEL),
            vec(D_MODEL),
            pl.BlockSpec(memory_space=pl.ANY),
            vec(ATTN_WIDTH),
            pl.BlockSpec((SSM_CONV, SSM_CONV_DIM), lambda i: (0, 0)),
            vec(SSM_CONV_DIM),
            w_rows(D_MODEL),
            w_rows(D_FF),
        ],
        out_specs=[cur(SSM_D_INNER), cur(SSM_CONV_DIM), cur(DT_PAD), prev(ATTN_WIDTH),
                   w_rows(D_MODEL), w_rows(D_FF)],
        out_shape=[
            jax.ShapeDtypeStruct((t, SSM_D_INNER), BF16),
            jax.ShapeDtypeStruct((t, SSM_CONV_DIM), BF16),
            jax.ShapeDtypeStruct((t, DT_PAD), F32),
            jax.ShapeDtypeStruct((t, ATTN_WIDTH), BF16),
            jax.ShapeDtypeStruct((D_MODEL, D_MODEL), BF16),
            jax.ShapeDtypeStruct((D_MODEL, D_FF), BF16),
        ],
        scratch_shapes=[pltpu.VMEM((n_cols, D_MODEL), BF16),
                        pltpu.VMEM((2, W_STAGE_ROWS, D_MODEL), F32),
                        pltpu.SemaphoreType.DMA((2,)),
                        pltpu.VMEM((SUBLANES + SSM_CHUNK, SSM_CONV_DIM), F32),
                        pltpu.VMEM((tm, ATTN_WIDTH + ATTN_KV_WIDTH), BF16),
                        pltpu.VMEM((tm, ATTN_WIDTH + ATTN_KV_WIDTH), BF16),
                        pltpu.VMEM((ATTN_BLOCK, ATTN_KV_WIDTH), BF16)],
        compiler_params=pltpu.CompilerParams(
            dimension_semantics=("arbitrary",), vmem_limit_bytes=VMEM_LIMIT_BIG),
        name="inproj_attn",
    )(sinks, x2d, g, w_in_t, attn_g, conv_w, conv_b, w_out, w_up)


def _conv_silu(xbc, seq_start, cw_ref, cb_ref, cbuf):
    Q = SSM_CHUNK
    cur = xbc.astype(F32)
    cbuf[SUBLANES:SUBLANES + Q, :] = cur
    cbuf[0:SUBLANES, :] = jnp.where(seq_start, 0.0, cbuf[0:SUBLANES, :])
    acc = cb_ref[...] + cw_ref[SSM_CONV - 1:SSM_CONV, :] * cur
    for k in range(SSM_CONV - 1):
        lo = SUBLANES - (SSM_CONV - 1) + k
        acc = acc + cw_ref[k:k + 1, :] * cbuf[lo:lo + Q, :]
    cbuf[0:SUBLANES, :] = cur[Q - SUBLANES:Q, :]
    return _silu(acc).astype(BF16)


def _ssd_chunk_stages(xc, z, dt_raw, dtb_ref, alog_ref, dskip_ref, ng_ref, state, emit):
    Q = SSM_CHUNK
    heads_per_group = SSM_N_HEADS // SSM_N_GROUPS
    dtr = dt_raw + dtb_ref[...]
    dt = jnp.maximum(dtr, 0.0) + jnp.log1p(jnp.exp(-jnp.abs(dtr)))
    a = dt * (-LOG2E * jnp.exp(alog_ref[...]))

    row = lax.broadcasted_iota(jnp.int32, (Q, Q), 0)
    col = lax.broadcasted_iota(jnp.int32, (Q, Q), 1)
    causal = row >= col
    tri = causal.astype(BF16)
    a_hi, a_mid, a_lo = _split3(a)
    a_cs = _dot(tri, a_hi) + _dot(tri, a_mid) + _dot(tri, a_lo)
    a_cs_t = a_cs.T
    yield

    xs = xc[:, :SSM_D_INNER].astype(F32)
    yield

    er = lax.broadcasted_iota(jnp.int32, (LANES, SSM_D_INNER), 0)
    ec = lax.broadcasted_iota(jnp.int32, (LANES, SSM_D_INNER), 1)
    expand = ((ec // SSM_HEAD_DIM) == er).astype(BF16)
    stacked = jnp.concatenate([dt, a_cs], axis=0)
    s_hi, s_mid, _ = _split3(stacked)
    ex = _dot(s_hi, expand) + _dot(s_mid, expand)
    dt_x = ex[:Q]
    acs_x = ex[Q:]
    acs_last = acs_x[Q - 1:Q, :]
    yield

    xd = xs * dt_x
    xd16 = xd.astype(BF16)
    xdd16 = (xd * jnp.exp2(acs_last - acs_x)).astype(BF16)
    exp_acs = jnp.exp2(acs_x)
    chunk_decay = jnp.exp2(acs_last)

    lane = lax.broadcasted_iota(jnp.int32, (Q, LANES), 1)
    first_half = lane < SSM_HEAD_DIM
    zero16 = jnp.zeros((Q, LANES), BF16)

    b16, c16, cb = [], [], []
    for g in range(SSM_N_GROUPS):
        b16.append(xc[:, SSM_D_INNER + g * SSM_D_STATE:
                      SSM_D_INNER + (g + 1) * SSM_D_STATE].astype(BF16))
        c16.append(xc[:, SSM_D_INNER + SSM_GN + g * SSM_D_STATE:
                      SSM_D_INNER + SSM_GN + (g + 1) * SSM_D_STATE].astype(BF16))
        cb.append(lax.dot_general(c16[g], b16[g], (((1,), (1,)), ((), ())),
                                  preferred_element_type=F32))
    yield

    y_off = []
    for g in range(SSM_N_GROUPS):
        gl = g * SSM_GROUP_WIDTH
        st = state[:, gl:gl + SSM_GROUP_WIDTH]
        y_off.append(_dot(c16[g], st.astype(BF16)) * exp_acs[:, gl:gl + SSM_GROUP_WIDTH])
        upd = lax.dot_general(b16[g], xdd16[:, gl:gl + SSM_GROUP_WIDTH],
                              (((0,), (0,)), ((), ())), preferred_element_type=F32)
        state[:, gl:gl + SSM_GROUP_WIDTH] = chunk_decay[:, gl:gl + SSM_GROUP_WIDTH] * st + upd
    yield

    y_parts = []
    for g in range(SSM_N_GROUPS):
        ms = []
        for r in range(heads_per_group):
            h = g * heads_per_group + r
            seg = a_cs[:, h:h + 1] - a_cs_t[h:h + 1, :]
            lmat = jnp.exp2(jnp.where(causal, seg, -jnp.inf))
            ms.append((cb[g] * lmat).astype(BF16))
        gl = g * SSM_GROUP_WIDTH
        yd = []
        for pr in range(2):
            xp = xd16[:, gl + pr * LANES:gl + (pr + 1) * LANES]
            rhs = jnp.concatenate([jnp.where(first_half, xp, zero16),
                                   jnp.where(first_half, zero16, xp)], axis=0)
            lhs = jnp.concatenate([ms[2 * pr], ms[2 * pr + 1]], axis=1)
            yd.append(_dot(lhs, rhs))
        y_parts.append(jnp.concatenate(yd, axis=1) + y_off[g])
        if g % 2 == 1:
            yield

    y = jnp.concatenate(y_parts, axis=1) + xs * dskip_ref[...]
    y = y * _silu(z.astype(F32))
    outs = []
    for g in range(SSM_N_GROUPS):
        yg = y[:, g * SSM_GROUP_WIDTH:(g + 1) * SSM_GROUP_WIDTH]
        outs.append(yg * _rms_scale(yg))
    emit(jnp.concatenate(outs, axis=1) * ng_ref[...])
    yield


def _ssd_outproj_kernel(xc_ref, z_ref, dt_ref, dtb_ref, alog_ref, dskip_ref,
                        ng_ref, x_ref, ya_ref, wo_ref, g2_ref, wd_ref,
                        x1_ref, h2_ref, wd16_ref,
                        state, ys_cur, ys_prev, *, tiles_per_seq):
    s = pl.program_id(0)
    n_tiles = pl.num_programs(0) - 1
    Q = SSM_CHUNK

    def cast_weights():
        wd16_ref[...] = wd_ref[...].astype(BF16)
        yield

    @pl.when((s % tiles_per_seq) == 0)
    def _():
        state[...] = jnp.zeros_like(state)

    def chunk(c):
        lo = c * Q

        def emit(y):
            ys_cur[lo:lo + Q, :] = y.astype(BF16)

        return _ssd_chunk_stages(
            xc_ref[lo:lo + Q, :], z_ref[lo:lo + Q, :], dt_ref[lo:lo + Q, :],
            dtb_ref, alog_ref, dskip_ref, ng_ref, state, emit)

    def project():
        y = jnp.concatenate([ys_prev[...], ya_ref[...]], axis=1)
        yield
        for lo in range(0, D_MODEL, MXU_WIDTH):
            x1_ref[:, lo:lo + MXU_WIDTH] = (x_ref[:, lo:lo + MXU_WIDTH]
                                            + _dot(y, wo_ref[:, lo:lo + MXU_WIDTH]))
            yield
        x1 = x1_ref[...]
        h2_ref[...] = (x1 * _rms_scale(x1) * g2_ref[...]).astype(BF16)
        yield

    assert TM_SSD == 2 * Q, "the trace patterns below are written for two chunks per tile"

    @pl.when(s == 0)
    def _():
        _interleave({"a": chunk(0), "c": chunk(1), "w": cast_weights()}, 8 * "a" + 8 * "c" + "w")

    @pl.when((s > 0) & (s < n_tiles))
    def _():
        ys_prev[...] = ys_cur[...]
        _interleave({"a": chunk(0), "c": chunk(1), "b": project(), "w": cast_weights()},
                    SCAN_PROJ_PATTERN)

    @pl.when(s == n_tiles)
    def _():
        ys_prev[...] = ys_cur[...]
        for _ in project():
            pass


def _ssd_outproj(x2d, xc, z, dt, y_att, dt_bias, a_log, d_skip, norm_g,
                 w_out16, mlp_g, w_down, seq_len):
    t = x2d.shape[0]
    tm = TM_SSD
    n_tiles = t // tm
    cur = lambda width: pl.BlockSpec((tm, width), lambda i: (jnp.minimum(i, n_tiles - 1), 0))
    prev = lambda width: pl.BlockSpec((tm, width), lambda i: (jnp.maximum(i - 1, 0), 0))
    full = lambda r, width: pl.BlockSpec((r, width), lambda i: (0, 0))
    part = lambda r, width: pl.BlockSpec((r, width), lambda i: (jnp.minimum(i, n_tiles - 1), 0))
    rd = D_FF // n_tiles
    return pl.pallas_call(
        functools.partial(_ssd_outproj_kernel, tiles_per_seq=seq_len // tm),
        grid=(n_tiles + 1,),
        in_specs=[cur(SSM_CONV_DIM), cur(SSM_D_INNER), cur(DT_PAD),
                  full(1, DT_PAD), full(1, DT_PAD),
                  full(1, SSM_D_INNER), full(1, SSM_D_INNER),
                  prev(D_MODEL), prev(ATTN_WIDTH),
                  pl.BlockSpec((D_MODEL, D_MODEL), lambda i: (0, 0), pipeline_mode=pl.Buffered(1)),
                  full(1, D_MODEL),
                  part(rd, D_MODEL)],
        out_specs=[prev(D_MODEL), prev(D_MODEL), part(rd, D_MODEL)],
        out_shape=[jax.ShapeDtypeStruct((t, D_MODEL), F32),
                   jax.ShapeDtypeStruct((t, D_MODEL), BF16),
                   jax.ShapeDtypeStruct((D_FF, D_MODEL), BF16)],
        scratch_shapes=[pltpu.VMEM((SSM_D_STATE, SSM_D_INNER), F32),
                        pltpu.VMEM((tm, SSM_D_INNER), BF16),
                        pltpu.VMEM((tm, SSM_D_INNER), BF16)],
        compiler_params=pltpu.CompilerParams(
            dimension_semantics=("arbitrary",), vmem_limit_bytes=VMEM_LIMIT),
        name="ssd_outproj",
    )(xc, z, dt, dt_bias, a_log, d_skip, norm_g,
      x2d, y_att, w_out16, mlp_g, w_down)


def _mlp_kernel(x1_ref, h2_ref, wu_ref, wd_ref, g_ref, o_ref):
    j = pl.program_id(1)

    @pl.when(j == 0)
    def _():
        o_ref[...] = x1_ref[...]

    u = jnp.maximum(_dot(h2_ref[...], wu_ref[...]), 0.0)
    o_ref[...] += _dot((u * u).astype(BF16), wd_ref[...])

    @pl.when(j == pl.num_programs(1) - 1)
    def _():
        x2 = o_ref[...]
        o_ref[...] = x2 * _rms_scale(x2) * g_ref[...]


def _mlp(x1, h2, w_up, w_down, g):
    t = x1.shape[0]
    tm, tf = TM_MLP, TF_MLP
    return pl.pallas_call(
        _mlp_kernel,
        grid=(t // tm, D_FF // tf),
        in_specs=[pl.BlockSpec((tm, D_MODEL), lambda i, j: (i, 0)),
                  pl.BlockSpec((tm, D_MODEL), lambda i, j: (i, 0)),
                  pl.BlockSpec((D_MODEL, tf), lambda i, j: (0, j)),
                  pl.BlockSpec((tf, D_MODEL), lambda i, j: (j, 0)),
                  pl.BlockSpec((1, D_MODEL), lambda i, j: (0, 0))],
        out_specs=pl.BlockSpec((tm, D_MODEL), lambda i, j: (i, 0)),
        out_shape=jax.ShapeDtypeStruct((t, D_MODEL), F32),
        compiler_params=pltpu.CompilerParams(
            dimension_semantics=("parallel", "arbitrary"), vmem_limit_bytes=VMEM_LIMIT_BIG),
        name="mlp",
    )(x1, h2, w_up, w_down, g)


def _row(v, width=None):
    v = v.astype(F32).reshape(1, -1)
    if width is not None and v.shape[1] < width:
        v = jnp.pad(v, ((0, 0), (0, width - v.shape[1])))
    return v


def kernel(x, mix_norm_g, w_in, conv_w, conv_b, dt_bias, A_log, D_skip, ssm_norm_g,
           attn_sinks, attn_out_norm_g, w_out, mlp_norm_g, w_up, w_down, final_norm_g):
    b, l, d = x.shape
    assert d == D_MODEL and l % SSM_CHUNK == 0 and l % TM_PROJ == 0
    assert w_in.shape[0] == 1, "one layer"
    x2d = x.reshape(b * l, d)

    z, xc, dt, y_att, w_out16, w_up16 = _inproj_attn(
        x2d, _row(mix_norm_g[0]), jnp.swapaxes(w_in.astype(F32), 1, 2),
        attn_sinks[0].astype(F32), _row(attn_out_norm_g[0]),
        conv_w[0].astype(F32), _row(conv_b[0]), w_out[0], w_up[0], l)

    x1, h2, w_down16 = _ssd_outproj(
        x2d, xc, z, dt, y_att,
        _row(dt_bias[0], DT_PAD), _row(A_log[0], DT_PAD),
        _row(jnp.repeat(D_skip[0], SSM_HEAD_DIM)), _row(ssm_norm_g[0]),
        w_out16, _row(mlp_norm_g[0]), w_down[0], l)
    out = _mlp(x1, h2, w_up16, w_down16, _row(final_norm_g))
    return out.reshape(b, l, d)
```

```python
import functools

import jax
import jax.numpy as jnp
from jax import lax
from jax.experimental import pallas as pl
from jax.experimental.pallas import tpu as pltpu

F32 = jnp.float32
BF16 = jnp.bfloat16

D_MODEL = 2048
SSM_D_INNER = 1024
SSM_HEAD_DIM = 64
SSM_N_HEADS = 16
SSM_N_GROUPS = 4
SSM_D_STATE = 128
SSM_CONV = 4
SSM_CHUNK = 128
SSM_GN = SSM_N_GROUPS * SSM_D_STATE
SSM_CONV_DIM = SSM_D_INNER + 2 * SSM_GN
SSM_GROUP_WIDTH = SSM_D_INNER // SSM_N_GROUPS
ATTN_WIDTH = 1024
ATTN_HEAD_DIM = 64
ATTN_N_HEADS = 16
ATTN_N_KV = 2
ATTN_KV_WIDTH = 2 * ATTN_N_KV * ATTN_HEAD_DIM
ATTN_BLOCK = 128
D_FF = 8192
EPS = 1e-5
LOG2E = 1.4426950408889634

LANES = 128
DT_PAD = LANES
SUBLANES = 8
MXU_WIDTH = 256
PROJ_ATTN_PATTERN = "abbbbc"
VMEM_LIMIT = 56 * 1024 * 1024
VMEM_LIMIT_BIG = 62 * 1024 * 1024

SCAN_CHUNK_KEYS = "acde"
SCAN_PROJ_PATTERN = "b" + "".join("abaaabaaaa".replace("a", k) for k in SCAN_CHUNK_KEYS) + "wb"

TM_PROJ = 512
W_STAGE_ROWS = 208
TM_SSD = 512
TM_MLP = 512
TF_MLP = 2048


def _dot(a, b):
    return jnp.dot(a, b, preferred_element_type=F32)


def _split3(x):
    hi = x.astype(BF16)
    r = x - hi.astype(F32)
    mid = r.astype(BF16)
    lo = (r - mid.astype(F32)).astype(BF16)
    return hi, mid, lo


def _rms_scale(x):
    return lax.rsqrt(jnp.mean(x * x, axis=-1, keepdims=True) + EPS)


def _silu(x):
    hx = 0.5 * x
    return hx + hx * jnp.tanh(hx)


def _interleave(gens, pattern):
    live = {key: True for key in gens}
    while any(live.values()):
        for key in pattern:
            if live[key]:
                live[key] = next(gens[key], _DONE) is not _DONE


_DONE = object()


def _attn_block_stages(q_blk, kv_prev, kv_cur, first_block, sink_ref, emit):
    QB = ATTN_BLOCK
    D = ATTN_HEAD_DIM
    kv = jnp.concatenate([kv_prev, kv_cur], axis=0).astype(F32)
    k01 = kv[:, :2 * D]
    v01 = kv[:, 2 * D:]
    k_t = k01.T
    v01r = pltpu.roll(v01, D, axis=1)

    lane_kv = lax.broadcasted_iota(jnp.int32, (2 * QB, LANES), 1)
    first_kv = lane_kv < D
    ones_e = first_kv.astype(BF16)
    ones_o = 1 - ones_e
    zero_t = jnp.zeros((D, 2 * QB), BF16)

    i = lax.broadcasted_iota(jnp.int32, (QB, 2 * QB), 0)
    j = lax.broadcasted_iota(jnp.int32, (QB, 2 * QB), 1)
    valid = (j > i) & (j <= i + QB) & ((j >= QB) | jnp.logical_not(first_block))
    lane_q = lax.broadcasted_iota(jnp.int32, (QB, LANES), 1)
    first_q = lane_q < D

    per_kv = ATTN_N_HEADS // ATTN_N_KV // 2
    rhs, vb = [], []
    for g in range(ATTN_N_KV):
        kg_t = k_t[g * D:(g + 1) * D, :].astype(BF16)
        rhs.append(jnp.concatenate([jnp.concatenate([kg_t, zero_t], axis=1),
                                    jnp.concatenate([zero_t, kg_t], axis=1)], axis=0))
        if g == 0:
            v_e = jnp.where(first_kv, v01, 0.0)
            v_o = jnp.where(first_kv, 0.0, v01r)
        else:
            v_e = jnp.where(first_kv, v01r, 0.0)
            v_o = jnp.where(first_kv, 0.0, v01)
        vb.append(jnp.concatenate(
            [jnp.concatenate([v_e.astype(BF16), ones_e], axis=1),
             jnp.concatenate([v_o.astype(BF16), ones_o], axis=1)], axis=0))
    yield

    def scores(jp):
        s = _dot(q_blk(jp), rhs[jp // per_kv]) * (D ** -0.5)
        s_e = jnp.where(valid, s[:, :2 * QB], -jnp.inf)
        s_o = jnp.where(valid, s[:, 2 * QB:], -jnp.inf)
        sink_e = sink_ref[2 * jp]
        sink_o = sink_ref[2 * jp + 1]
        m_e = jnp.maximum(jnp.max(s_e, axis=-1, keepdims=True), sink_e)
        m_o = jnp.maximum(jnp.max(s_o, axis=-1, keepdims=True), sink_o)
        p = jnp.concatenate([jnp.exp(s_e - m_e), jnp.exp(s_o - m_o)], axis=1).astype(BF16)
        sink_term = jnp.where(first_q, jnp.exp(sink_e - m_e), jnp.exp(sink_o - m_o))
        return p, sink_term

    def values(jp, p, sink_term):
        o = _dot(p, vb[jp // per_kv])
        return o[:, :LANES] / (o[:, LANES:] + sink_term)

    n_pairs = ATTN_N_KV * per_kv
    outs = []
    pending = scores(0)
    yield
    for jp in range(1, n_pairs):
        nxt = scores(jp)
        yield
        outs.append(values(jp - 1, *pending))
        pending = nxt
        yield
    outs.append(values(n_pairs - 1, *pending))
    emit(jnp.concatenate(outs, axis=1))
    yield


def _inproj_attn_kernel(sink_ref, x_ref, g_ref, wt_hbm, ng_ref, wo_ref, wu_ref,
                        z_ref, xbc_ref, dt_ref, yatt_ref, wo16_ref, wu16_ref,
                        wt_ref, stage, stage_sem,
                        cur_scr, prev_scr, kvtail_scr, *, tiles_per_seq):
    s = pl.program_id(0)
    n_tiles = pl.num_programs(0) - 1
    blocks = TM_PROJ // ATTN_BLOCK
    tail = slice(TM_PROJ - ATTN_BLOCK, TM_PROJ)

    def cast_weights():
        wo16_ref[...] = wo_ref[...].astype(BF16)
        yield
        for lo in range(0, D_FF, D_MODEL):
            wu16_ref[:, lo:lo + D_MODEL] = wu_ref[:, lo:lo + D_MODEL].astype(BF16)
            yield

    def stage_copy(r, slot):
        return pltpu.make_async_copy(wt_hbm.at[0, pl.ds(r * W_STAGE_ROWS, W_STAGE_ROWS), :],
                                     stage.at[slot], stage_sem.at[slot])

    def load_weights():
        n_chunks = wt_ref.shape[0] // W_STAGE_ROWS
        stage_copy(0, 0).start()
        for r in range(n_chunks):
            slot = r % 2
            if r + 1 < n_chunks:
                stage_copy(r + 1, 1 - slot).start()
            stage_copy(r, slot).wait()
            wt_ref[r * W_STAGE_ROWS:(r + 1) * W_STAGE_ROWS, :] = stage[slot].astype(BF16)

    def project():
        x = x_ref[...]
        h = (x * _rms_scale(x) * g_ref[...]).astype(BF16)
        yield

        def cols(lo, width):
            return lax.dot_general(h, wt_ref[lo:lo + width, :], (((1,), (1,)), ((), ())),
                                   preferred_element_type=F32)

        o_dt = SSM_D_INNER + SSM_CONV_DIM
        o_q = o_dt + SSM_N_HEADS
        for dst, base, width in ((z_ref, 0, SSM_D_INNER), (xbc_ref, SSM_D_INNER, SSM_CONV_DIM),
                                 (cur_scr, o_q, ATTN_WIDTH + ATTN_KV_WIDTH)):
            for lo in range(0, width, MXU_WIDTH):
                dst[:, lo:lo + MXU_WIDTH] = cols(base + lo, MXU_WIDTH).astype(BF16)
                yield
        lane = lax.broadcasted_iota(jnp.int32, (TM_PROJ, DT_PAD), 1)
        dt_ref[...] = jnp.where(lane < SSM_N_HEADS, cols(o_dt, DT_PAD), 0.0)
        yield

    def rotate_buffers():
        kvtail_scr[...] = prev_scr[tail, ATTN_WIDTH:]
        prev_scr[...] = cur_scr[...]

    def attend():
        seq_start = ((s - 1) % tiles_per_seq) == 0
        for r in range(blocks):
            lo = r * ATTN_BLOCK
            kv_cur = prev_scr[lo:lo + ATTN_BLOCK, ATTN_WIDTH:]
            if r == 0:
                kv_prev = kvtail_scr[...]
                first_block = seq_start
            else:
                kv_prev = prev_scr[lo - ATTN_BLOCK:lo, ATTN_WIDTH:]
                first_block = False
            q_blk = lambda jp, lo=lo: prev_scr[lo:lo + ATTN_BLOCK, jp * LANES:(jp + 1) * LANES]

            def emit(y, lo=lo):
                yatt_ref[lo:lo + ATTN_BLOCK, :] = (y * _rms_scale(y) * ng_ref[...]).astype(BF16)

            yield from _attn_block_stages(q_blk, kv_prev, kv_cur, first_block, sink_ref, emit)

    @pl.when(s == 0)
    def _():
        prev_scr[tail, ATTN_WIDTH:] = jnp.zeros((ATTN_BLOCK, ATTN_KV_WIDTH), BF16)
        load_weights()
        _interleave({"a": project(), "c": cast_weights()}, "ac")

    @pl.when((s > 0) & (s < n_tiles))
    def _():
        rotate_buffers()
        _interleave({"a": project(), "b": attend(), "c": cast_weights()}, PROJ_ATTN_PATTERN)

    @pl.when(s == n_tiles)
    def _():
        rotate_buffers()
        for _ in attend():
            pass


def _inproj_attn(x2d, g, w_in_t, sinks, attn_g, w_out, w_up, seq_len):
    t = x2d.shape[0]
    n_cols = w_in_t.shape[1]
    assert n_cols % W_STAGE_ROWS == 0 and w_in_t.shape[2] == D_MODEL
    tm = TM_PROJ
    n_tiles = t // tm
    cur = lambda width: pl.BlockSpec((tm, width), lambda i: (jnp.minimum(i, n_tiles - 1), 0))
    prev = lambda width: pl.BlockSpec((tm, width), lambda i: (jnp.maximum(i - 1, 0), 0))
    vec = lambda width: pl.BlockSpec((1, width), lambda i: (0, 0))
    w_rows = lambda width: pl.BlockSpec((D_MODEL // n_tiles, width),
                                        lambda i: (jnp.minimum(i, n_tiles - 1), 0))
    return pl.pallas_call(
        functools.partial(_inproj_attn_kernel, tiles_per_seq=seq_len // tm),
        grid=(n_tiles + 1,),
        in_specs=[
            pl.BlockSpec(memory_space=pltpu.SMEM),
            cur(D_MODEL),
            vec(D_MODEL),
            pl.BlockSpec(memory_space=pl.ANY),
            vec(ATTN_WIDTH),
            w_rows(D_MODEL),
            w_rows(D_FF),
        ],
        out_specs=[cur(SSM_D_INNER), cur(SSM_CONV_DIM), cur(DT_PAD), prev(ATTN_WIDTH),
                   w_rows(D_MODEL), w_rows(D_FF)],
        out_shape=[
            jax.ShapeDtypeStruct((t, SSM_D_INNER), BF16),
            jax.ShapeDtypeStruct((t, SSM_CONV_DIM), BF16),
            jax.ShapeDtypeStruct((t, DT_PAD), F32),
            jax.ShapeDtypeStruct((t, ATTN_WIDTH), BF16),
            jax.ShapeDtypeStruct((D_MODEL, D_MODEL), BF16),
            jax.ShapeDtypeStruct((D_MODEL, D_FF), BF16),
        ],
        scratch_shapes=[pltpu.VMEM((n_cols, D_MODEL), BF16),
                        pltpu.VMEM((2, W_STAGE_ROWS, D_MODEL), F32),
                        pltpu.SemaphoreType.DMA((2,)),
                        pltpu.VMEM((tm, ATTN_WIDTH + ATTN_KV_WIDTH), BF16),
                        pltpu.VMEM((tm, ATTN_WIDTH + ATTN_KV_WIDTH), BF16),
                        pltpu.VMEM((ATTN_BLOCK, ATTN_KV_WIDTH), BF16)],
        compiler_params=pltpu.CompilerParams(
            dimension_semantics=("arbitrary",), vmem_limit_bytes=VMEM_LIMIT_BIG),
        name="inproj_attn",
    )(sinks, x2d, g, w_in_t, attn_g, w_out, w_up)


def _ssd_chunk_stages(xbc, z, dt_raw, cw_ref, cb_ref, dtb_ref, alog_ref, dskip_ref, ng_ref,
                      cbuf, state, emit):
    Q = SSM_CHUNK
    heads_per_group = SSM_N_HEADS // SSM_N_GROUPS
    dtr = dt_raw + dtb_ref[...]
    dt = jnp.maximum(dtr, 0.0) + jnp.log1p(jnp.exp(-jnp.abs(dtr)))
    a = dt * (-LOG2E * jnp.exp(alog_ref[...]))

    row = lax.broadcasted_iota(jnp.int32, (Q, Q), 0)
    col = lax.broadcasted_iota(jnp.int32, (Q, Q), 1)
    causal = row >= col
    tri = causal.astype(BF16)
    a_hi, a_mid, a_lo = _split3(a)
    a_cs = _dot(tri, a_hi) + _dot(tri, a_mid) + _dot(tri, a_lo)
    a_cs_t = a_cs.T
    yield

    cur = xbc.astype(F32)
    cbuf[SUBLANES:SUBLANES + Q, :] = cur
    acc = cb_ref[...] + cw_ref[SSM_CONV - 1:SSM_CONV, :] * cur
    for k in range(SSM_CONV - 1):
        lo = SUBLANES - (SSM_CONV - 1) + k
        acc = acc + cw_ref[k:k + 1, :] * cbuf[lo:lo + Q, :]
    cbuf[0:SUBLANES, :] = cur[Q - SUBLANES:Q, :]
    xc = _silu(acc)
    xs = xc[:, :SSM_D_INNER]
    yield

    er = lax.broadcasted_iota(jnp.int32, (LANES, SSM_D_INNER), 0)
    ec = lax.broadcasted_iota(jnp.int32, (LANES, SSM_D_INNER), 1)
    expand = ((ec // SSM_HEAD_DIM) == er).astype(BF16)
    stacked = jnp.concatenate([dt, a_cs], axis=0)
    s_hi, s_mid, _ = _split3(stacked)
    ex = _dot(s_hi, expand) + _dot(s_mid, expand)
    dt_x = ex[:Q]
    acs_x = ex[Q:]
    acs_last = acs_x[Q - 1:Q, :]
    yield

    xd = xs * dt_x
    xd16 = xd.astype(BF16)
    xdd16 = (xd * jnp.exp2(acs_last - acs_x)).astype(BF16)
    exp_acs = jnp.exp2(acs_x)
    chunk_decay = jnp.exp2(acs_last)

    lane = lax.broadcasted_iota(jnp.int32, (Q, LANES), 1)
    first_half = lane < SSM_HEAD_DIM
    zero16 = jnp.zeros((Q, LANES), BF16)

    b16, c16, cb = [], [], []
    for g in range(SSM_N_GROUPS):
        b16.append(xc[:, SSM_D_INNER + g * SSM_D_STATE:
                      SSM_D_INNER + (g + 1) * SSM_D_STATE].astype(BF16))
        c16.append(xc[:, SSM_D_INNER + SSM_GN + g * SSM_D_STATE:
                      SSM_D_INNER + SSM_GN + (g + 1) * SSM_D_STATE].astype(BF16))
        cb.append(lax.dot_general(c16[g], b16[g], (((1,), (1,)), ((), ())),
                                  preferred_element_type=F32))
    yield

    y_off = []
    for g in range(SSM_N_GROUPS):
        gl = g * SSM_GROUP_WIDTH
        st = state[:, gl:gl + SSM_GROUP_WIDTH]
        y_off.append(_dot(c16[g], st.astype(BF16)) * exp_acs[:, gl:gl + SSM_GROUP_WIDTH])
        upd = lax.dot_general(b16[g], xdd16[:, gl:gl + SSM_GROUP_WIDTH],
                              (((0,), (0,)), ((), ())), preferred_element_type=F32)
        state[:, gl:gl + SSM_GROUP_WIDTH] = chunk_decay[:, gl:gl + SSM_GROUP_WIDTH] * st + upd
    yield

    y_parts = []
    for g in range(SSM_N_GROUPS):
        ms = []
        for r in range(heads_per_group):
            h = g * heads_per_group + r
            seg = a_cs[:, h:h + 1] - a_cs_t[h:h + 1, :]
            lmat = jnp.exp2(jnp.where(causal, seg, -jnp.inf))
            ms.append((cb[g] * lmat).astype(BF16))
        gl = g * SSM_GROUP_WIDTH
        yd = []
        for pr in range(2):
            xp = xd16[:, gl + pr * LANES:gl + (pr + 1) * LANES]
            rhs = jnp.concatenate([jnp.where(first_half, xp, zero16),
                                   jnp.where(first_half, zero16, xp)], axis=0)
            lhs = jnp.concatenate([ms[2 * pr], ms[2 * pr + 1]], axis=1)
            yd.append(_dot(lhs, rhs))
        y_parts.append(jnp.concatenate(yd, axis=1) + y_off[g])
        if g % 2 == 1:
            yield

    y = jnp.concatenate(y_parts, axis=1) + xs * dskip_ref[...]
    y = y * _silu(z.astype(F32))
    outs = []
    for g in range(SSM_N_GROUPS):
        yg = y[:, g * SSM_GROUP_WIDTH:(g + 1) * SSM_GROUP_WIDTH]
        outs.append(yg * _rms_scale(yg))
    emit(jnp.concatenate(outs, axis=1) * ng_ref[...])
    yield


def _ssd_outproj_kernel(xbc_ref, z_ref, dt_ref, cw_ref, cb_ref, dtb_ref, alog_ref, dskip_ref,
                        ng_ref, x_ref, ya_ref, wo_ref, g2_ref, wd_ref,
                        x1_ref, h2_ref, wd16_ref,
                        cbuf, state, ys_cur, ys_prev, *, tiles_per_seq):
    s = pl.program_id(0)
    n_tiles = pl.num_programs(0) - 1
    Q = SSM_CHUNK

    def cast_weights():
        wd16_ref[...] = wd_ref[...].astype(BF16)
        yield

    @pl.when((s % tiles_per_seq) == 0)
    def _():
        cbuf[0:SUBLANES, :] = jnp.zeros((SUBLANES, SSM_CONV_DIM), F32)
        state[...] = jnp.zeros_like(state)

    def chunk(c):
        lo = c * Q

        def emit(y):
            ys_cur[lo:lo + Q, :] = y.astype(BF16)

        return _ssd_chunk_stages(
            xbc_ref[lo:lo + Q, :], z_ref[lo:lo + Q, :], dt_ref[lo:lo + Q, :],
            cw_ref, cb_ref, dtb_ref, alog_ref, dskip_ref, ng_ref, cbuf, state, emit)

    def project():
        y = jnp.concatenate([ys_prev[...], ya_ref[...]], axis=1)
        yield
        for lo in range(0, D_MODEL, MXU_WIDTH):
            x1_ref[:, lo:lo + MXU_WIDTH] = (x_ref[:, lo:lo + MXU_WIDTH]
                                            + _dot(y, wo_ref[:, lo:lo + MXU_WIDTH]))
            yield
        x1 = x1_ref[...]
        h2_ref[...] = (x1 * _rms_scale(x1) * g2_ref[...]).astype(BF16)
        yield

    assert TM_SSD == len(SCAN_CHUNK_KEYS) * Q, "one trace-pattern key per chunk of the tile"

    def chunks():
        return {key: chunk(c) for c, key in enumerate(SCAN_CHUNK_KEYS)}

    @pl.when(s == 0)
    def _():
        _interleave({**chunks(), "w": cast_weights()},
                    "".join(8 * key for key in SCAN_CHUNK_KEYS) + "w")

    @pl.when((s > 0) & (s < n_tiles))
    def _():
        ys_prev[...] = ys_cur[...]
        _interleave({**chunks(), "b": project(), "w": cast_weights()}, SCAN_PROJ_PATTERN)

    @pl.when(s == n_tiles)
    def _():
        ys_prev[...] = ys_cur[...]
        for _ in project():
            pass


def _ssd_outproj(x2d, xbc, z, dt, y_att, conv_w, conv_b, dt_bias, a_log, d_skip, norm_g,
                 w_out16, mlp_g, w_down, seq_len):
    t = x2d.shape[0]
    tm = TM_SSD
    n_tiles = t // tm
    cur = lambda width: pl.BlockSpec((tm, width), lambda i: (jnp.minimum(i, n_tiles - 1), 0))
    prev = lambda width: pl.BlockSpec((tm, width), lambda i: (jnp.maximum(i - 1, 0), 0))
    full = lambda r, width: pl.BlockSpec((r, width), lambda i: (0, 0))
    part = lambda r, width: pl.BlockSpec((r, width), lambda i: (jnp.minimum(i, n_tiles - 1), 0))
    rd = D_FF // n_tiles
    return pl.pallas_call(
        functools.partial(_ssd_outproj_kernel, tiles_per_seq=seq_len // tm),
        grid=(n_tiles + 1,),
        in_specs=[cur(SSM_CONV_DIM), cur(SSM_D_INNER), cur(DT_PAD),
                  full(SSM_CONV, SSM_CONV_DIM), full(1, SSM_CONV_DIM),
                  full(1, DT_PAD), full(1, DT_PAD),
                  full(1, SSM_D_INNER), full(1, SSM_D_INNER),
                  prev(D_MODEL), prev(ATTN_WIDTH),
                  pl.BlockSpec((D_MODEL, D_MODEL), lambda i: (0, 0), pipeline_mode=pl.Buffered(1)),
                  full(1, D_MODEL),
                  part(rd, D_MODEL)],
        out_specs=[prev(D_MODEL), prev(D_MODEL), part(rd, D_MODEL)],
        out_shape=[jax.ShapeDtypeStruct((t, D_MODEL), F32),
                   jax.ShapeDtypeStruct((t, D_MODEL), BF16),
                   jax.ShapeDtypeStruct((D_FF, D_MODEL), BF16)],
        scratch_shapes=[pltpu.VMEM((SUBLANES + SSM_CHUNK, SSM_CONV_DIM), F32),
                        pltpu.VMEM((SSM_D_STATE, SSM_D_INNER), F32),
                        pltpu.VMEM((tm, SSM_D_INNER), BF16),
                        pltpu.VMEM((tm, SSM_D_INNER), BF16)],
        compiler_params=pltpu.CompilerParams(
            dimension_semantics=("arbitrary",), vmem_limit_bytes=VMEM_LIMIT_BIG),
        name="ssd_outproj",
    )(xbc, z, dt, conv_w, conv_b, dt_bias, a_log, d_skip, norm_g,
      x2d, y_att, w_out16, mlp_g, w_down)


def _mlp_kernel(x1_ref, h2_ref, wu_ref, wd_ref, g_ref, o_ref):
    j = pl.program_id(1)

    @pl.when(j == 0)
    def _():
        o_ref[...] = x1_ref[...]

    u = jnp.maximum(_dot(h2_ref[...], wu_ref[...]), 0.0)
    o_ref[...] += _dot((u * u).astype(BF16), wd_ref[...])

    @pl.when(j == pl.num_programs(1) - 1)
    def _():
        x2 = o_ref[...]
        o_ref[...] = x2 * _rms_scale(x2) * g_ref[...]


def _mlp(x1, h2, w_up, w_down, g):
    t = x1.shape[0]
    tm, tf = TM_MLP, TF_MLP
    return pl.pallas_call(
        _mlp_kernel,
        grid=(t // tm, D_FF // tf),
        in_specs=[pl.BlockSpec((tm, D_MODEL), lambda i, j: (i, 0)),
                  pl.BlockSpec((tm, D_MODEL), lambda i, j: (i, 0)),
                  pl.BlockSpec((D_MODEL, tf), lambda i, j: (0, j)),
                  pl.BlockSpec((tf, D_MODEL), lambda i, j: (j, 0)),
                  pl.BlockSpec((1, D_MODEL), lambda i, j: (0, 0))],
        out_specs=pl.BlockSpec((tm, D_MODEL), lambda i, j: (i, 0)),
        out_shape=jax.ShapeDtypeStruct((t, D_MODEL), F32),
        compiler_params=pltpu.CompilerParams(
            dimension_semantics=("parallel", "arbitrary"), vmem_limit_bytes=VMEM_LIMIT_BIG),
        name="mlp",
    )(x1, h2, w_up, w_down, g)


def _row(v, width=None):
    v = v.astype(F32).reshape(1, -1)
    if width is not None and v.shape[1] < width:
        v = jnp.pad(v, ((0, 0), (0, width - v.shape[1])))
    return v


def kernel(x, mix_norm_g, w_in, conv_w, conv_b, dt_bias, A_log, D_skip, ssm_norm_g,
           attn_sinks, attn_out_norm_g, w_out, mlp_norm_g, w_up, w_down, final_norm_g):
    b, l, d = x.shape
    assert d == D_MODEL and l % SSM_CHUNK == 0 and l % TM_PROJ == 0
    assert w_in.shape[0] == 1, "one layer"
    x2d = x.reshape(b * l, d)

    z, xbc, dt, y_att, w_out16, w_up16 = _inproj_attn(
        x2d, _row(mix_norm_g[0]), jnp.swapaxes(w_in.astype(F32), 1, 2),
        attn_sinks[0].astype(F32), _row(attn_out_norm_g[0]), w_out[0], w_up[0], l)

    x1, h2, w_down16 = _ssd_outproj(
        x2d, xbc, z, dt, y_att,
        conv_w[0].astype(F32), _row(conv_b[0]),
        _row(dt_bias[0], DT_PAD), _row(A_log[0], DT_PAD),
        _row(jnp.repeat(D_skip[0], SSM_HEAD_DIM)), _row(ssm_norm_g[0]),
        w_out16, _row(mlp_norm_g[0]), w_down[0], l)
    out = _mlp(x1, h2, w_up16, w_down16, _row(final_norm_g))
    return out.reshape(b, l, d)
```

```python
import functools

import jax
import jax.numpy as jnp
from jax import lax
from jax.experimental import pallas as pl
from jax.experimental.pallas import tpu as pltpu

F32 = jnp.float32
BF16 = jnp.bfloat16

D_MODEL = 2048
SSM_D_INNER = 1024
SSM_HEAD_DIM = 64
SSM_N_HEADS = 16
SSM_N_GROUPS = 4
SSM_D_STATE = 128
SSM_CONV = 4
SSM_CHUNK = 128
SSM_GN = SSM_N_GROUPS * SSM_D_STATE
SSM_CONV_DIM = SSM_D_INNER + 2 * SSM_GN
SSM_GROUP_WIDTH = SSM_D_INNER // SSM_N_GROUPS
ATTN_WIDTH = 1024
ATTN_HEAD_DIM = 64
ATTN_N_HEADS = 16
ATTN_N_KV = 2
ATTN_KV_WIDTH = 2 * ATTN_N_KV * ATTN_HEAD_DIM
ATTN_BLOCK = 128
D_FF = 8192
EPS = 1e-5
LOG2E = 1.4426950408889634

LANES = 128
DT_PAD = LANES
SUBLANES = 8
MXU_WIDTH = 256
PROJ_ATTN_PATTERN = "abbbbc"
VMEM_LIMIT = 56 * 1024 * 1024
VMEM_LIMIT_BIG = 62 * 1024 * 1024

SCAN_PROJ_PATTERN = "b" "abaababaabaa" "cbccbcbccbcc" "wb"

TM_PROJ = 512
W_STAGE_ROWS = 208
TM_SSD = 256
TM_MLP = 512
TF_MLP = 2048


def _dot(a, b):
    return jnp.dot(a, b, preferred_element_type=F32)


def _split3(x):
    hi = x.astype(BF16)
    r = x - hi.astype(F32)
    mid = r.astype(BF16)
    lo = (r - mid.astype(F32)).astype(BF16)
    return hi, mid, lo


def _rms_scale(x):
    return lax.rsqrt(jnp.mean(x * x, axis=-1, keepdims=True) + EPS)


def _silu(x):
    hx = 0.5 * x
    return hx + hx * jnp.tanh(hx)


def _interleave(gens, pattern):
    live = {key: True for key in gens}
    while any(live.values()):
        for key in pattern:
            if live[key]:
                live[key] = next(gens[key], _DONE) is not _DONE


_DONE = object()


def _attn_block_stages(q_blk, kv_prev, kv_cur, first_block, sink_ref, emit):
    QB = ATTN_BLOCK
    D = ATTN_HEAD_DIM
    kv = jnp.concatenate([kv_prev, kv_cur], axis=0).astype(F32)
    k01 = kv[:, :2 * D]
    v01 = kv[:, 2 * D:]
    k_t = k01.T
    v01r = pltpu.roll(v01, D, axis=1)

    lane_kv = lax.broadcasted_iota(jnp.int32, (2 * QB, LANES), 1)
    first_kv = lane_kv < D
    ones_e = first_kv.astype(BF16)
    ones_o = 1 - ones_e
    zero_t = jnp.zeros((D, 2 * QB), BF16)

    i = lax.broadcasted_iota(jnp.int32, (QB, 2 * QB), 0)
    j = lax.broadcasted_iota(jnp.int32, (QB, 2 * QB), 1)
    valid = (j > i) & (j <= i + QB) & ((j >= QB) | jnp.logical_not(first_block))
    lane_q = lax.broadcasted_iota(jnp.int32, (QB, LANES), 1)
    first_q = lane_q < D

    per_kv = ATTN_N_HEADS // ATTN_N_KV // 2
    rhs, vb = [], []
    for g in range(ATTN_N_KV):
        kg_t = k_t[g * D:(g + 1) * D, :].astype(BF16)
        rhs.append(jnp.concatenate([jnp.concatenate([kg_t, zero_t], axis=1),
                                    jnp.concatenate([zero_t, kg_t], axis=1)], axis=0))
        if g == 0:
            v_e = jnp.where(first_kv, v01, 0.0)
            v_o = jnp.where(first_kv, 0.0, v01r)
        else:
            v_e = jnp.where(first_kv, v01r, 0.0)
            v_o = jnp.where(first_kv, 0.0, v01)
        vb.append(jnp.concatenate(
            [jnp.concatenate([v_e.astype(BF16), ones_e], axis=1),
             jnp.concatenate([v_o.astype(BF16), ones_o], axis=1)], axis=0))
    yield

    def scores(jp):
        s = _dot(q_blk(jp), rhs[jp // per_kv]) * (D ** -0.5)
        s_e = jnp.where(valid, s[:, :2 * QB], -jnp.inf)
        s_o = jnp.where(valid, s[:, 2 * QB:], -jnp.inf)
        sink_e = sink_ref[2 * jp]
        sink_o = sink_ref[2 * jp + 1]
        m_e = jnp.maximum(jnp.max(s_e, axis=-1, keepdims=True), sink_e)
        m_o = jnp.maximum(jnp.max(s_o, axis=-1, keepdims=True), sink_o)
        p = jnp.concatenate([jnp.exp(s_e - m_e), jnp.exp(s_o - m_o)], axis=1).astype(BF16)
        sink_term = jnp.where(first_q, jnp.exp(sink_e - m_e), jnp.exp(sink_o - m_o))
        return p, sink_term

    def values(jp, p, sink_term):
        o = _dot(p, vb[jp // per_kv])
        return o[:, :LANES] / (o[:, LANES:] + sink_term)

    n_pairs = ATTN_N_KV * per_kv
    outs = []
    pending = scores(0)
    yield
    for jp in range(1, n_pairs):
        nxt = scores(jp)
        yield
        outs.append(values(jp - 1, *pending))
        pending = nxt
        yield
    outs.append(values(n_pairs - 1, *pending))
    emit(jnp.concatenate(outs, axis=1))
    yield


def _inproj_attn_kernel(sink_ref, x_ref, g_ref, wt_hbm, ng_ref, wu_ref,
                        z_ref, xbc_ref, dt_ref, yatt_ref, wu16_ref,
                        wt_ref, stage, stage_sem,
                        cur_scr, prev_scr, kvtail_scr, *, tiles_per_seq):
    s = pl.program_id(0)
    n_tiles = pl.num_programs(0) - 1
    blocks = TM_PROJ // ATTN_BLOCK
    tail = slice(TM_PROJ - ATTN_BLOCK, TM_PROJ)

    def cast_weights():
        for lo in range(0, D_FF, D_MODEL):
            wu16_ref[:, lo:lo + D_MODEL] = wu_ref[:, lo:lo + D_MODEL].astype(BF16)
            yield

    def stage_copy(r, slot):
        return pltpu.make_async_copy(wt_hbm.at[0, pl.ds(r * W_STAGE_ROWS, W_STAGE_ROWS), :],
                                     stage.at[slot], stage_sem.at[slot])

    def load_weights():
        n_chunks = wt_ref.shape[0] // W_STAGE_ROWS
        stage_copy(0, 0).start()
        for r in range(n_chunks):
            slot = r % 2
            if r + 1 < n_chunks:
                stage_copy(r + 1, 1 - slot).start()
            stage_copy(r, slot).wait()
            wt_ref[r * W_STAGE_ROWS:(r + 1) * W_STAGE_ROWS, :] = stage[slot].astype(BF16)

    def project():
        x = x_ref[...]
        h = (x * _rms_scale(x) * g_ref[...]).astype(BF16)
        yield

        def cols(lo, width):
            return lax.dot_general(h, wt_ref[lo:lo + width, :], (((1,), (1,)), ((), ())),
                                   preferred_element_type=F32)

        o_dt = SSM_D_INNER + SSM_CONV_DIM
        o_q = o_dt + SSM_N_HEADS
        for dst, base, width in ((z_ref, 0, SSM_D_INNER), (xbc_ref, SSM_D_INNER, SSM_CONV_DIM),
                                 (cur_scr, o_q, ATTN_WIDTH + ATTN_KV_WIDTH)):
            for lo in range(0, width, MXU_WIDTH):
                dst[:, lo:lo + MXU_WIDTH] = cols(base + lo, MXU_WIDTH).astype(BF16)
                yield
        lane = lax.broadcasted_iota(jnp.int32, (TM_PROJ, DT_PAD), 1)
        dt_ref[...] = jnp.where(lane < SSM_N_HEADS, cols(o_dt, DT_PAD), 0.0)
        yield

    def rotate_buffers():
        kvtail_scr[...] = prev_scr[tail, ATTN_WIDTH:]
        prev_scr[...] = cur_scr[...]

    def attend():
        seq_start = ((s - 1) % tiles_per_seq) == 0
        for r in range(blocks):
            lo = r * ATTN_BLOCK
            kv_cur = prev_scr[lo:lo + ATTN_BLOCK, ATTN_WIDTH:]
            if r == 0:
                kv_prev = kvtail_scr[...]
                first_block = seq_start
            else:
                kv_prev = prev_scr[lo - ATTN_BLOCK:lo, ATTN_WIDTH:]
                first_block = False
            q_blk = lambda jp, lo=lo: prev_scr[lo:lo + ATTN_BLOCK, jp * LANES:(jp + 1) * LANES]

            def emit(y, lo=lo):
                yatt_ref[lo:lo + ATTN_BLOCK, :] = (y * _rms_scale(y) * ng_ref[...]).astype(BF16)

            yield from _attn_block_stages(q_blk, kv_prev, kv_cur, first_block, sink_ref, emit)

    @pl.when(s == 0)
    def _():
        prev_scr[tail, ATTN_WIDTH:] = jnp.zeros((ATTN_BLOCK, ATTN_KV_WIDTH), BF16)
        load_weights()
        _interleave({"a": project(), "c": cast_weights()}, "ac")

    @pl.when((s > 0) & (s < n_tiles))
    def _():
        rotate_buffers()
        _interleave({"a": project(), "b": attend(), "c": cast_weights()}, PROJ_ATTN_PATTERN)

    @pl.when(s == n_tiles)
    def _():
        rotate_buffers()
        for _ in attend():
            pass


def _inproj_attn(x2d, g, w_in_t, sinks, attn_g, w_up, seq_len):
    t = x2d.shape[0]
    n_cols = w_in_t.shape[1]
    assert n_cols % W_STAGE_ROWS == 0 and w_in_t.shape[2] == D_MODEL
    tm = TM_PROJ
    n_tiles = t // tm
    cur = lambda width: pl.BlockSpec((tm, width), lambda i: (jnp.minimum(i, n_tiles - 1), 0))
    prev = lambda width: pl.BlockSpec((tm, width), lambda i: (jnp.maximum(i - 1, 0), 0))
    vec = lambda width: pl.BlockSpec((1, width), lambda i: (0, 0))
    w_rows = lambda width: pl.BlockSpec((D_MODEL // n_tiles, width),
                                        lambda i: (jnp.minimum(i, n_tiles - 1), 0))
    return pl.pallas_call(
        functools.partial(_inproj_attn_kernel, tiles_per_seq=seq_len // tm),
        grid=(n_tiles + 1,),
        in_specs=[
            pl.BlockSpec(memory_space=pltpu.SMEM),
            cur(D_MODEL),
            vec(D_MODEL),
            pl.BlockSpec(memory_space=pl.ANY),
            vec(ATTN_WIDTH),
            w_rows(D_FF),
        ],
        out_specs=[cur(SSM_D_INNER), cur(SSM_CONV_DIM), cur(DT_PAD), prev(ATTN_WIDTH),
                   w_rows(D_FF)],
        out_shape=[
            jax.ShapeDtypeStruct((t, SSM_D_INNER), BF16),
            jax.ShapeDtypeStruct((t, SSM_CONV_DIM), BF16),
            jax.ShapeDtypeStruct((t, DT_PAD), F32),
            jax.ShapeDtypeStruct((t, ATTN_WIDTH), BF16),
            jax.ShapeDtypeStruct((D_MODEL, D_FF), BF16),
        ],
        scratch_shapes=[pltpu.VMEM((n_cols, D_MODEL), BF16),
                        pltpu.VMEM((2, W_STAGE_ROWS, D_MODEL), F32),
                        pltpu.SemaphoreType.DMA((2,)),
                        pltpu.VMEM((tm, ATTN_WIDTH + ATTN_KV_WIDTH), BF16),
                        pltpu.VMEM((tm, ATTN_WIDTH + ATTN_KV_WIDTH), BF16),
                        pltpu.VMEM((ATTN_BLOCK, ATTN_KV_WIDTH), BF16)],
        compiler_params=pltpu.CompilerParams(
            dimension_semantics=("arbitrary",), vmem_limit_bytes=VMEM_LIMIT_BIG),
        name="inproj_attn",
    )(sinks, x2d, g, w_in_t, attn_g, w_up)


def _ssd_chunk_stages(xbc, z, dt_raw, cw_ref, cb_ref, dtb_ref, alog_ref, dskip_ref, ng_ref,
                      cbuf, state, emit):
    Q = SSM_CHUNK
    heads_per_group = SSM_N_HEADS // SSM_N_GROUPS
    dtr = dt_raw + dtb_ref[...]
    dt = jnp.maximum(dtr, 0.0) + jnp.log1p(jnp.exp(-jnp.abs(dtr)))
    a = dt * (-LOG2E * jnp.exp(alog_ref[...]))

    row = lax.broadcasted_iota(jnp.int32, (Q, Q), 0)
    col = lax.broadcasted_iota(jnp.int32, (Q, Q), 1)
    causal = row >= col
    tri = causal.astype(BF16)
    a_hi, a_mid, a_lo = _split3(a)
    a_cs = _dot(tri, a_hi) + _dot(tri, a_mid) + _dot(tri, a_lo)
    a_cs_t = a_cs.T
    yield

    cur = xbc.astype(F32)
    cbuf[SUBLANES:SUBLANES + Q, :] = cur
    acc = cb_ref[...] + cw_ref[SSM_CONV - 1:SSM_CONV, :] * cur
    for k in range(SSM_CONV - 1):
        lo = SUBLANES - (SSM_CONV - 1) + k
        acc = acc + cw_ref[k:k + 1, :] * cbuf[lo:lo + Q, :]
    cbuf[0:SUBLANES, :] = cur[Q - SUBLANES:Q, :]
    xc = _silu(acc)
    xs = xc[:, :SSM_D_INNER]
    yield

    er = lax.broadcasted_iota(jnp.int32, (LANES, SSM_D_INNER), 0)
    ec = lax.broadcasted_iota(jnp.int32, (LANES, SSM_D_INNER), 1)
    expand = ((ec // SSM_HEAD_DIM) == er).astype(BF16)
    stacked = jnp.concatenate([dt, a_cs], axis=0)
    s_hi, s_mid, _ = _split3(stacked)
    ex = _dot(s_hi, expand) + _dot(s_mid, expand)
    dt_x = ex[:Q]
    acs_x = ex[Q:]
    acs_last = acs_x[Q - 1:Q, :]
    yield

    xd = xs * dt_x
    xd16 = xd.astype(BF16)
    xdd16 = (xd * jnp.exp2(acs_last - acs_x)).astype(BF16)
    exp_acs = jnp.exp2(acs_x)
    chunk_decay = jnp.exp2(acs_last)

    lane = lax.broadcasted_iota(jnp.int32, (Q, LANES), 1)
    first_half = lane < SSM_HEAD_DIM
    zero16 = jnp.zeros((Q, LANES), BF16)

    b16, c16, cb = [], [], []
    for g in range(SSM_N_GROUPS):
        b16.append(xc[:, SSM_D_INNER + g * SSM_D_STATE:
                      SSM_D_INNER + (g + 1) * SSM_D_STATE].astype(BF16))
        c16.append(xc[:, SSM_D_INNER + SSM_GN + g * SSM_D_STATE:
                      SSM_D_INNER + SSM_GN + (g + 1) * SSM_D_STATE].astype(BF16))
        cb.append(lax.dot_general(c16[g], b16[g], (((1,), (1,)), ((), ())),
                                  preferred_element_type=F32))
    yield

    y_off = []
    for g in range(SSM_N_GROUPS):
        gl = g * SSM_GROUP_WIDTH
        st = state[:, gl:gl + SSM_GROUP_WIDTH]
        y_off.append(_dot(c16[g], st.astype(BF16)) * exp_acs[:, gl:gl + SSM_GROUP_WIDTH])
        upd = lax.dot_general(b16[g], xdd16[:, gl:gl + SSM_GROUP_WIDTH],
                              (((0,), (0,)), ((), ())), preferred_element_type=F32)
        state[:, gl:gl + SSM_GROUP_WIDTH] = chunk_decay[:, gl:gl + SSM_GROUP_WIDTH] * st + upd
    yield

    y_parts = []
    for g in range(SSM_N_GROUPS):
        ms = []
        for r in range(heads_per_group):
            h = g * heads_per_group + r
            seg = a_cs[:, h:h + 1] - a_cs_t[h:h + 1, :]
            lmat = jnp.exp2(jnp.where(causal, seg, -jnp.inf))
            ms.append((cb[g] * lmat).astype(BF16))
        gl = g * SSM_GROUP_WIDTH
        yd = []
        for pr in range(2):
            xp = xd16[:, gl + pr * LANES:gl + (pr + 1) * LANES]
            rhs = jnp.concatenate([jnp.where(first_half, xp, zero16),
                                   jnp.where(first_half, zero16, xp)], axis=0)
            lhs = jnp.concatenate([ms[2 * pr], ms[2 * pr + 1]], axis=1)
            yd.append(_dot(lhs, rhs))
        y_parts.append(jnp.concatenate(yd, axis=1) + y_off[g])
        if g % 2 == 1:
            yield

    y = jnp.concatenate(y_parts, axis=1) + xs * dskip_ref[...]
    y = y * _silu(z.astype(F32))
    outs = []
    for g in range(SSM_N_GROUPS):
        yg = y[:, g * SSM_GROUP_WIDTH:(g + 1) * SSM_GROUP_WIDTH]
        outs.append(yg * _rms_scale(yg))
    emit(jnp.concatenate(outs, axis=1) * ng_ref[...])
    yield


def _ssd_outproj_kernel(xbc_ref, z_ref, dt_ref, cw_ref, cb_ref, dtb_ref, alog_ref, dskip_ref,
                        ng_ref, x_ref, ya_ref, wo_ref, g2_ref, wd_ref,
                        x1_ref, h2_ref, wd16_ref,
                        cbuf, state, ys_cur, ys_prev, *, tiles_per_seq):
    s = pl.program_id(0)
    n_tiles = pl.num_programs(0) - 1
    Q = SSM_CHUNK

    def cast_weights():
        wd16_ref[...] = wd_ref[...].astype(BF16)
        yield

    @pl.when((s % tiles_per_seq) == 0)
    def _():
        cbuf[0:SUBLANES, :] = jnp.zeros((SUBLANES, SSM_CONV_DIM), F32)
        state[...] = jnp.zeros_like(state)

    def chunk(c):
        lo = c * Q

        def emit(y):
            ys_cur[lo:lo + Q, :] = y.astype(BF16)

        return _ssd_chunk_stages(
            xbc_ref[lo:lo + Q, :], z_ref[lo:lo + Q, :], dt_ref[lo:lo + Q, :],
            cw_ref, cb_ref, dtb_ref, alog_ref, dskip_ref, ng_ref, cbuf, state, emit)

    def project():
        y = jnp.concatenate([ys_prev[...], ya_ref[...]], axis=1)
        yield
        for lo in range(0, D_MODEL, MXU_WIDTH):
            x1_ref[:, lo:lo + MXU_WIDTH] = (x_ref[:, lo:lo + MXU_WIDTH]
                                            + _dot(y, wo_ref[:, lo:lo + MXU_WIDTH]))
            yield
        x1 = x1_ref[...]
        h2_ref[...] = (x1 * _rms_scale(x1) * g2_ref[...]).astype(BF16)
        yield

    assert TM_SSD == 2 * Q, "the trace patterns below are written for two chunks per tile"

    @pl.when(s == 0)
    def _():
        _interleave({"a": chunk(0), "c": chunk(1), "w": cast_weights()}, 8 * "a" + 8 * "c" + "w")

    @pl.when((s > 0) & (s < n_tiles))
    def _():
        ys_prev[...] = ys_cur[...]
        _interleave({"a": chunk(0), "c": chunk(1), "b": project(), "w": cast_weights()},
                    SCAN_PROJ_PATTERN)

    @pl.when(s == n_tiles)
    def _():
        ys_prev[...] = ys_cur[...]
        for _ in project():
            pass


def _ssd_outproj(x2d, xbc, z, dt, y_att, conv_w, conv_b, dt_bias, a_log, d_skip, norm_g,
                 w_out16, mlp_g, w_down, seq_len):
    t = x2d.shape[0]
    tm = TM_SSD
    n_tiles = t // tm
    cur = lambda width: pl.BlockSpec((tm, width), lambda i: (jnp.minimum(i, n_tiles - 1), 0))
    prev = lambda width: pl.BlockSpec((tm, width), lambda i: (jnp.maximum(i - 1, 0), 0))
    full = lambda r, width: pl.BlockSpec((r, width), lambda i: (0, 0))
    part = lambda r, width: pl.BlockSpec((r, width), lambda i: (jnp.minimum(i, n_tiles - 1), 0))
    rd = D_FF // n_tiles
    return pl.pallas_call(
        functools.partial(_ssd_outproj_kernel, tiles_per_seq=seq_len // tm),
        grid=(n_tiles + 1,),
        in_specs=[cur(SSM_CONV_DIM), cur(SSM_D_INNER), cur(DT_PAD),
                  full(SSM_CONV, SSM_CONV_DIM), full(1, SSM_CONV_DIM),
                  full(1, DT_PAD), full(1, DT_PAD),
                  full(1, SSM_D_INNER), full(1, SSM_D_INNER),
                  prev(D_MODEL), prev(ATTN_WIDTH),
                  pl.BlockSpec((D_MODEL, D_MODEL), lambda i: (0, 0), pipeline_mode=pl.Buffered(1)),
                  full(1, D_MODEL),
                  part(rd, D_MODEL)],
        out_specs=[prev(D_MODEL), prev(D_MODEL), part(rd, D_MODEL)],
        out_shape=[jax.ShapeDtypeStruct((t, D_MODEL), F32),
                   jax.ShapeDtypeStruct((t, D_MODEL), BF16),
                   jax.ShapeDtypeStruct((D_FF, D_MODEL), BF16)],
        scratch_shapes=[pltpu.VMEM((SUBLANES + SSM_CHUNK, SSM_CONV_DIM), F32),
                        pltpu.VMEM((SSM_D_STATE, SSM_D_INNER), F32),
                        pltpu.VMEM((tm, SSM_D_INNER), BF16),
                        pltpu.VMEM((tm, SSM_D_INNER), BF16)],
        compiler_params=pltpu.CompilerParams(
            dimension_semantics=("arbitrary",), vmem_limit_bytes=VMEM_LIMIT),
        name="ssd_outproj",
    )(xbc, z, dt, conv_w, conv_b, dt_bias, a_log, d_skip, norm_g,
      x2d, y_att, w_out16, mlp_g, w_down)


def _mlp_kernel(x1_ref, h2_ref, wu_ref, wd_ref, g_ref, o_ref):
    j = pl.program_id(1)
    last = pl.num_programs(1) - 1

    def step(first, final):
        u = jnp.maximum(_dot(h2_ref[...], wu_ref[...]), 0.0)
        base = x1_ref[...] if first else o_ref[...]
        x2 = base + _dot((u * u).astype(BF16), wd_ref[...])
        o_ref[...] = x2 * _rms_scale(x2) * g_ref[...] if final else x2

    @pl.when(j == 0)
    def _():
        step(True, False)

    @pl.when((j > 0) & (j < last))
    def _():
        step(False, False)

    @pl.when(j == last)
    def _():
        step(False, True)


def _mlp(x1, h2, w_up, w_down, g):
    t = x1.shape[0]
    tm, tf = TM_MLP, TF_MLP
    assert D_FF // tf >= 2, "the kernel treats the first and last d_ff blocks separately"
    return pl.pallas_call(
        _mlp_kernel,
        grid=(t // tm, D_FF // tf),
        in_specs=[pl.BlockSpec((tm, D_MODEL), lambda i, j: (i, 0)),
                  pl.BlockSpec((tm, D_MODEL), lambda i, j: (i, 0)),
                  pl.BlockSpec((D_MODEL, tf), lambda i, j: (0, j)),
                  pl.BlockSpec((tf, D_MODEL), lambda i, j: (j, 0)),
                  pl.BlockSpec((1, D_MODEL), lambda i, j: (0, 0))],
        out_specs=pl.BlockSpec((tm, D_MODEL), lambda i, j: (i, 0)),
        out_shape=jax.ShapeDtypeStruct((t, D_MODEL), F32),
        compiler_params=pltpu.CompilerParams(
            dimension_semantics=("parallel", "arbitrary"), vmem_limit_bytes=VMEM_LIMIT_BIG),
        name="mlp",
    )(x1, h2, w_up, w_down, g)


def _row(v, width=None):
    v = v.astype(F32).reshape(1, -1)
    if width is not None and v.shape[1] < width:
        v = jnp.pad(v, ((0, 0), (0, width - v.shape[1])))
    return v


def kernel(x, mix_norm_g, w_in, conv_w, conv_b, dt_bias, A_log, D_skip, ssm_norm_g,
           attn_sinks, attn_out_norm_g, w_out, mlp_norm_g, w_up, w_down, final_norm_g):
    b, l, d = x.shape
    assert d == D_MODEL and l % SSM_CHUNK == 0 and l % TM_PROJ == 0
    assert w_in.shape[0] == 1, "one layer"
    x2d = x.reshape(b * l, d)

    z, xbc, dt, y_att, w_up16 = _inproj_attn(
        x2d, _row(mix_norm_g[0]), jnp.swapaxes(w_in.astype(F32), 1, 2),
        attn_sinks[0].astype(F32), _row(attn_out_norm_g[0]), w_up[0], l)
    w_out16 = w_out[0].astype(BF16)

    x1, h2, w_down16 = _ssd_outproj(
        x2d, xbc, z, dt, y_att,
        conv_w[0].astype(F32), _row(conv_b[0]),
        _row(dt_bias[0], DT_PAD), _row(A_log[0], DT_PAD),
        _row(jnp.repeat(D_skip[0], SSM_HEAD_DIM)), _row(ssm_norm_g[0]),
        w_out16, _row(mlp_norm_g[0]), w_down[0], l)
    out = _mlp(x1, h2, w_up16, w_down16, _row(final_norm_g))
    return out.reshape(b, l, d)
```

```python
import functools

import jax
import jax.numpy as jnp
from jax import lax
from jax.experimental import pallas as pl
from jax.experimental.pallas import tpu as pltpu

F32 = jnp.float32
BF16 = jnp.bfloat16

D_MODEL = 2048
SSM_D_INNER = 1024
SSM_HEAD_DIM = 64
SSM_N_HEADS = 16
SSM_N_GROUPS = 4
SSM_D_STATE = 128
SSM_CONV = 4
SSM_CHUNK = 128
SSM_GN = SSM_N_GROUPS * SSM_D_STATE
SSM_CONV_DIM = SSM_D_INNER + 2 * SSM_GN
SSM_GROUP_WIDTH = SSM_D_INNER // SSM_N_GROUPS
ATTN_WIDTH = 1024
ATTN_HEAD_DIM = 64
ATTN_N_HEADS = 16
ATTN_N_KV = 2
ATTN_KV_WIDTH = 2 * ATTN_N_KV * ATTN_HEAD_DIM
ATTN_BLOCK = 128
D_FF = 8192
EPS = 1e-5
LOG2E = 1.4426950408889634

LANES = 128
DT_PAD = LANES
SUBLANES = 8
MXU_WIDTH = 256
PROJ_ATTN_PATTERN = "abbbbc"
VMEM_LIMIT = 56 * 1024 * 1024
VMEM_LIMIT_BIG = 62 * 1024 * 1024

SCAN_PROJ_PATTERN = "b" "abaababaabaa" "cbccbcbccbcc" "wb"

TM_PROJ = 512
W_STAGE_ROWS = 208
TM_SSD = 256
TM_MLP = 512
TF_MLP = 2048


def _dot(a, b):
    return jnp.dot(a, b, preferred_element_type=F32)


def _split3(x):
    hi = x.astype(BF16)
    r = x - hi.astype(F32)
    mid = r.astype(BF16)
    lo = (r - mid.astype(F32)).astype(BF16)
    return hi, mid, lo


def _rms_scale(x):
    return lax.rsqrt(jnp.mean(x * x, axis=-1, keepdims=True) + EPS)


def _silu(x):
    hx = 0.5 * x
    return hx + hx * jnp.tanh(hx)


def _interleave(gens, pattern):
    live = {key: True for key in gens}
    while any(live.values()):
        for key in pattern:
            if live[key]:
                live[key] = next(gens[key], _DONE) is not _DONE


_DONE = object()


def _attn_block_stages(q_blk, kv_prev, kv_cur, first_block, sink_ref, emit):
    QB = ATTN_BLOCK
    D = ATTN_HEAD_DIM
    kv = jnp.concatenate([kv_prev, kv_cur], axis=0).astype(F32)
    k01 = kv[:, :2 * D]
    v01 = kv[:, 2 * D:]
    k_t = k01.T
    v01r = pltpu.roll(v01, D, axis=1)

    lane_kv = lax.broadcasted_iota(jnp.int32, (2 * QB, LANES), 1)
    first_kv = lane_kv < D
    ones_e = first_kv.astype(BF16)
    ones_o = 1 - ones_e
    zero_t = jnp.zeros((D, 2 * QB), BF16)

    i = lax.broadcasted_iota(jnp.int32, (QB, 2 * QB), 0)
    j = lax.broadcasted_iota(jnp.int32, (QB, 2 * QB), 1)
    valid = (j > i) & (j <= i + QB) & ((j >= QB) | jnp.logical_not(first_block))
    lane_q = lax.broadcasted_iota(jnp.int32, (QB, LANES), 1)
    first_q = lane_q < D

    per_kv = ATTN_N_HEADS // ATTN_N_KV // 2
    rhs, vb = [], []
    for g in range(ATTN_N_KV):
        kg_t = k_t[g * D:(g + 1) * D, :].astype(BF16)
        rhs.append(jnp.concatenate([jnp.concatenate([kg_t, zero_t], axis=1),
                                    jnp.concatenate([zero_t, kg_t], axis=1)], axis=0))
        if g == 0:
            v_e = jnp.where(first_kv, v01, 0.0)
            v_o = jnp.where(first_kv, 0.0, v01r)
        else:
            v_e = jnp.where(first_kv, v01r, 0.0)
            v_o = jnp.where(first_kv, 0.0, v01)
        vb.append(jnp.concatenate(
            [jnp.concatenate([v_e.astype(BF16), ones_e], axis=1),
             jnp.concatenate([v_o.astype(BF16), ones_o], axis=1)], axis=0))
    yield

    def scores(jp):
        s = _dot(q_blk(jp), rhs[jp // per_kv]) * (D ** -0.5)
        s_e = jnp.where(valid, s[:, :2 * QB], -jnp.inf)
        s_o = jnp.where(valid, s[:, 2 * QB:], -jnp.inf)
        sink_e = sink_ref[2 * jp]
        sink_o = sink_ref[2 * jp + 1]
        m_e = jnp.maximum(jnp.max(s_e, axis=-1, keepdims=True), sink_e)
        m_o = jnp.maximum(jnp.max(s_o, axis=-1, keepdims=True), sink_o)
        p = jnp.concatenate([jnp.exp(s_e - m_e), jnp.exp(s_o - m_o)], axis=1).astype(BF16)
        sink_term = jnp.where(first_q, jnp.exp(sink_e - m_e), jnp.exp(sink_o - m_o))
        return p, sink_term

    def values(jp, p, sink_term):
        o = _dot(p, vb[jp // per_kv])
        return o[:, :LANES] / (o[:, LANES:] + sink_term)

    n_pairs = ATTN_N_KV * per_kv
    outs = []
    pending = scores(0)
    yield
    for jp in range(1, n_pairs):
        nxt = scores(jp)
        yield
        outs.append(values(jp - 1, *pending))
        pending = nxt
        yield
    outs.append(values(n_pairs - 1, *pending))
    emit(jnp.concatenate(outs, axis=1))
    yield


def _inproj_attn_kernel(sink_ref, x_ref, g_ref, wt_hbm, ng_ref, wo_ref, wu_ref,
                        z_ref, xbc_ref, dt_ref, yatt_ref, wo16_ref, wu16_ref,
                        wt_ref, cur_scr, prev_scr, kvtail_scr, stage, stage_sem,
                        *, tiles_per_seq):
    s = pl.program_id(0)
    n_tiles = pl.num_programs(0) - 1
    blocks = TM_PROJ // ATTN_BLOCK
    tail = slice(TM_PROJ - ATTN_BLOCK, TM_PROJ)

    def cast_weights():
        wo16_ref[...] = wo_ref[...].astype(BF16)
        yield
        for lo in range(0, D_FF, D_MODEL):
            wu16_ref[:, lo:lo + D_MODEL] = wu_ref[:, lo:lo + D_MODEL].astype(BF16)
            yield

    def stage_copy(r, slot):
        return pltpu.make_async_copy(wt_hbm.at[0, pl.ds(r * W_STAGE_ROWS, W_STAGE_ROWS), :],
                                     stage.at[slot], stage_sem.at[slot])

    def load_weights():
        n_chunks = wt_ref.shape[0] // W_STAGE_ROWS
        stage_copy(0, 0).start()
        for r in range(n_chunks):
            slot = r % 2
            if r + 1 < n_chunks:
                stage_copy(r + 1, 1 - slot).start()
            stage_copy(r, slot).wait()
            wt_ref[r * W_STAGE_ROWS:(r + 1) * W_STAGE_ROWS, :] = stage[slot].astype(BF16)

    def project():
        x = x_ref[...]
        h = (x * _rms_scale(x) * g_ref[...]).astype(BF16)
        yield

        def cols(lo, width):
            return lax.dot_general(h, wt_ref[lo:lo + width, :], (((1,), (1,)), ((), ())),
                                   preferred_element_type=F32)

        o_dt = SSM_D_INNER + SSM_CONV_DIM
        o_q = o_dt + SSM_N_HEADS
        for dst, base, width in ((z_ref, 0, SSM_D_INNER), (xbc_ref, SSM_D_INNER, SSM_CONV_DIM),
                                 (cur_scr, o_q, ATTN_WIDTH + ATTN_KV_WIDTH)):
            for lo in range(0, width, MXU_WIDTH):
                dst[:, lo:lo + MXU_WIDTH] = cols(base + lo, MXU_WIDTH).astype(BF16)
                yield
        lane = lax.broadcasted_iota(jnp.int32, (TM_PROJ, DT_PAD), 1)
        dt_ref[...] = jnp.where(lane < SSM_N_HEADS, cols(o_dt, DT_PAD), 0.0)
        yield

    def rotate_buffers():
        kvtail_scr[...] = prev_scr[tail, ATTN_WIDTH:]
        prev_scr[...] = cur_scr[...]

    def attend():
        seq_start = ((s - 1) % tiles_per_seq) == 0
        for r in range(blocks):
            lo = r * ATTN_BLOCK
            kv_cur = prev_scr[lo:lo + ATTN_BLOCK, ATTN_WIDTH:]
            if r == 0:
                kv_prev = kvtail_scr[...]
                first_block = seq_start
            else:
                kv_prev = prev_scr[lo - ATTN_BLOCK:lo, ATTN_WIDTH:]
                first_block = False
            q_blk = lambda jp, lo=lo: prev_scr[lo:lo + ATTN_BLOCK, jp * LANES:(jp + 1) * LANES]

            def emit(y, lo=lo):
                yatt_ref[lo:lo + ATTN_BLOCK, :] = (y * _rms_scale(y) * ng_ref[...]).astype(BF16)

            yield from _attn_block_stages(q_blk, kv_prev, kv_cur, first_block, sink_ref, emit)

    @pl.when(s == 0)
    def _():
        prev_scr[tail, ATTN_WIDTH:] = jnp.zeros((ATTN_BLOCK, ATTN_KV_WIDTH), BF16)
        load_weights()
        _interleave({"a": project(), "c": cast_weights()}, "ac")

    @pl.when((s > 0) & (s < n_tiles))
    def _():
        rotate_buffers()
        _interleave({"a": project(), "b": attend(), "c": cast_weights()}, PROJ_ATTN_PATTERN)

    @pl.when(s == n_tiles)
    def _():
        rotate_buffers()
        for _ in attend():
            pass


def _inproj_attn(x2d, g, w_in_t, sinks, attn_g, w_out, w_up, seq_len):
    t = x2d.shape[0]
    n_cols = w_in_t.shape[1]
    assert n_cols % W_STAGE_ROWS == 0 and w_in_t.shape[2] == D_MODEL
    tm = TM_PROJ
    n_tiles = t // tm
    cur = lambda width: pl.BlockSpec((tm, width), lambda i: (jnp.minimum(i, n_tiles - 1), 0))
    prev = lambda width: pl.BlockSpec((tm, width), lambda i: (jnp.maximum(i - 1, 0), 0))
    vec = lambda width: pl.BlockSpec((1, width), lambda i: (0, 0))
    w_rows = lambda width: pl.BlockSpec((D_MODEL // n_tiles, width),
                                        lambda i: (jnp.minimum(i, n_tiles - 1), 0))
    return pl.pallas_call(
        functools.partial(_inproj_attn_kernel, tiles_per_seq=seq_len // tm),
        grid=(n_tiles + 1,),
        in_specs=[
            pl.BlockSpec(memory_space=pltpu.SMEM),
            cur(D_MODEL),
            vec(D_MODEL),
            pl.BlockSpec(memory_space=pl.ANY),
            vec(ATTN_WIDTH),
            w_rows(D_MODEL),
            w_rows(D_FF),
        ],
        out_specs=[cur(SSM_D_INNER), cur(SSM_CONV_DIM), cur(DT_PAD), prev(ATTN_WIDTH),
                   w_rows(D_MODEL), w_rows(D_FF)],
        out_shape=[
            jax.ShapeDtypeStruct((t, SSM_D_INNER), BF16),
            jax.ShapeDtypeStruct((t, SSM_CONV_DIM), BF16),
            jax.ShapeDtypeStruct((t, DT_PAD), F32),
            jax.ShapeDtypeStruct((t, ATTN_WIDTH), BF16),
            jax.ShapeDtypeStruct((D_MODEL, D_MODEL), BF16),
            jax.ShapeDtypeStruct((D_MODEL, D_FF), BF16),
        ],
        scratch_shapes=[pltpu.VMEM((n_cols, D_MODEL), BF16),
                        pltpu.VMEM((tm, ATTN_WIDTH + ATTN_KV_WIDTH), BF16),
                        pltpu.VMEM((tm, ATTN_WIDTH + ATTN_KV_WIDTH), BF16),
                        pltpu.VMEM((ATTN_BLOCK, ATTN_KV_WIDTH), BF16),
                        pltpu.VMEM((2, W_STAGE_ROWS, D_MODEL), F32),
                        pltpu.SemaphoreType.DMA((2,))],
        compiler_params=pltpu.CompilerParams(
            dimension_semantics=("arbitrary",), vmem_limit_bytes=VMEM_LIMIT_BIG),
        name="inproj_attn",
    )(sinks, x2d, g, w_in_t, attn_g, w_out, w_up)


def _ssd_chunk_stages(xbc, z, dt_raw, cw_ref, cb_ref, dtb_ref, alog_ref, dskip_ref, ng_ref,
                      cbuf, state, emit):
    Q = SSM_CHUNK
    heads_per_group = SSM_N_HEADS // SSM_N_GROUPS
    dtr = dt_raw + dtb_ref[...]
    dt = jnp.maximum(dtr, 0.0) + jnp.log1p(jnp.exp(-jnp.abs(dtr)))
    a = dt * (-LOG2E * jnp.exp(alog_ref[...]))

    row = lax.broadcasted_iota(jnp.int32, (Q, Q), 0)
    col = lax.broadcasted_iota(jnp.int32, (Q, Q), 1)
    causal = row >= col
    tri = causal.astype(BF16)
    a_hi, a_mid, a_lo = _split3(a)
    a_cs = _dot(tri, a_hi) + _dot(tri, a_mid) + _dot(tri, a_lo)
    a_cs_t = a_cs.T
    yield

    cur = xbc.astype(F32)
    cbuf[SUBLANES:SUBLANES + Q, :] = cur
    acc = cb_ref[...] + cw_ref[SSM_CONV - 1:SSM_CONV, :] * cur
    for k in range(SSM_CONV - 1):
        lo = SUBLANES - (SSM_CONV - 1) + k
        acc = acc + cw_ref[k:k + 1, :] * cbuf[lo:lo + Q, :]
    cbuf[0:SUBLANES, :] = cur[Q - SUBLANES:Q, :]
    xc = _silu(acc)
    xs = xc[:, :SSM_D_INNER]
    yield

    er = lax.broadcasted_iota(jnp.int32, (LANES, SSM_D_INNER), 0)
    ec = lax.broadcasted_iota(jnp.int32, (LANES, SSM_D_INNER), 1)
    expand = ((ec // SSM_HEAD_DIM) == er).astype(BF16)
    stacked = jnp.concatenate([dt, a_cs], axis=0)
    s_hi, s_mid, _ = _split3(stacked)
    ex = _dot(s_hi, expand) + _dot(s_mid, expand)
    dt_x = ex[:Q]
    acs_x = ex[Q:]
    acs_last = acs_x[Q - 1:Q, :]
    yield

    xd = xs * dt_x
    xd16 = xd.astype(BF16)
    xdd16 = (xd * jnp.exp2(acs_last - acs_x)).astype(BF16)
    exp_acs = jnp.exp2(acs_x)
    chunk_decay = jnp.exp2(acs_last)

    lane = lax.broadcasted_iota(jnp.int32, (Q, LANES), 1)
    first_half = lane < SSM_HEAD_DIM
    zero16 = jnp.zeros((Q, LANES), BF16)

    b16, c16, cb = [], [], []
    for g in range(SSM_N_GROUPS):
        b16.append(xc[:, SSM_D_INNER + g * SSM_D_STATE:
                      SSM_D_INNER + (g + 1) * SSM_D_STATE].astype(BF16))
        c16.append(xc[:, SSM_D_INNER + SSM_GN + g * SSM_D_STATE:
                      SSM_D_INNER + SSM_GN + (g + 1) * SSM_D_STATE].astype(BF16))
        cb.append(lax.dot_general(c16[g], b16[g], (((1,), (1,)), ((), ())),
                                  preferred_element_type=F32))
    yield

    y_off = []
    for g in range(SSM_N_GROUPS):
        gl = g * SSM_GROUP_WIDTH
        st = state[:, gl:gl + SSM_GROUP_WIDTH]
        y_off.append(_dot(c16[g], st.astype(BF16)) * exp_acs[:, gl:gl + SSM_GROUP_WIDTH])
        upd = lax.dot_general(b16[g], xdd16[:, gl:gl + SSM_GROUP_WIDTH],
                              (((0,), (0,)), ((), ())), preferred_element_type=F32)
        state[:, gl:gl + SSM_GROUP_WIDTH] = chunk_decay[:, gl:gl + SSM_GROUP_WIDTH] * st + upd
    yield

    y_parts = []
    for g in range(SSM_N_GROUPS):
        ms = []
        for r in range(heads_per_group):
            h = g * heads_per_group + r
            seg = a_cs[:, h:h + 1] - a_cs_t[h:h + 1, :]
            lmat = jnp.exp2(jnp.where(causal, seg, -jnp.inf))
            ms.append((cb[g] * lmat).astype(BF16))
        gl = g * SSM_GROUP_WIDTH
        yd = []
        for pr in range(2):
            xp = xd16[:, gl + pr * LANES:gl + (pr + 1) * LANES]
            rhs = jnp.concatenate([jnp.where(first_half, xp, zero16),
                                   jnp.where(first_half, zero16, xp)], axis=0)
            lhs = jnp.concatenate([ms[2 * pr], ms[2 * pr + 1]], axis=1)
            yd.append(_dot(lhs, rhs))
        y_parts.append(jnp.concatenate(yd, axis=1) + y_off[g])
        if g % 2 == 1:
            yield

    y = jnp.concatenate(y_parts, axis=1) + xs * dskip_ref[...]
    y = y * _silu(z.astype(F32))
    outs = []
    for g in range(SSM_N_GROUPS):
        yg = y[:, g * SSM_GROUP_WIDTH:(g + 1) * SSM_GROUP_WIDTH]
        outs.append(yg * _rms_scale(yg))
    emit(jnp.concatenate(outs, axis=1) * ng_ref[...])
    yield


def _ssd_outproj_kernel(xbc_ref, z_ref, dt_ref, cw_ref, cb_ref, dtb_ref, alog_ref, dskip_ref,
                        ng_ref, x_ref, ya_ref, wo_ref, g2_ref, wd_ref,
                        x1_ref, h2_ref, wd16_ref,
                        cbuf, state, ys_cur, ys_prev, *, tiles_per_seq):
    s = pl.program_id(0)
    n_tiles = pl.num_programs(0) - 1
    Q = SSM_CHUNK

    def cast_weights():
        wd16_ref[...] = wd_ref[...].astype(BF16)
        yield

    @pl.when((s % tiles_per_seq) == 0)
    def _():
        cbuf[0:SUBLANES, :] = jnp.zeros((SUBLANES, SSM_CONV_DIM), F32)
        state[...] = jnp.zeros_like(state)

    def chunk(c):
        lo = c * Q

        def emit(y):
            ys_cur[lo:lo + Q, :] = y.astype(BF16)

        return _ssd_chunk_stages(
            xbc_ref[lo:lo + Q, :], z_ref[lo:lo + Q, :], dt_ref[lo:lo + Q, :],
            cw_ref, cb_ref, dtb_ref, alog_ref, dskip_ref, ng_ref, cbuf, state, emit)

    def project():
        y = jnp.concatenate([ys_prev[...], ya_ref[...]], axis=1)
        yield
        for lo in range(0, D_MODEL, MXU_WIDTH):
            x1_ref[:, lo:lo + MXU_WIDTH] = (x_ref[:, lo:lo + MXU_WIDTH]
                                            + _dot(y, wo_ref[:, lo:lo + MXU_WIDTH]))
            yield
        x1 = x1_ref[...]
        h2_ref[...] = (x1 * _rms_scale(x1) * g2_ref[...]).astype(BF16)
        yield

    assert TM_SSD == 2 * Q, "the trace patterns below are written for two chunks per tile"

    @pl.when(s == 0)
    def _():
        _interleave({"a": chunk(0), "c": chunk(1), "w": cast_weights()}, 8 * "a" + 8 * "c" + "w")

    @pl.when((s > 0) & (s < n_tiles))
    def _():
        ys_prev[...] = ys_cur[...]
        _interleave({"a": chunk(0), "c": chunk(1), "b": project(), "w": cast_weights()},
                    SCAN_PROJ_PATTERN)

    @pl.when(s == n_tiles)
    def _():
        ys_prev[...] = ys_cur[...]
        for _ in project():
            pass


def _ssd_outproj(x2d, xbc, z, dt, y_att, conv_w, conv_b, dt_bias, a_log, d_skip, norm_g,
                 w_out16, mlp_g, w_down, seq_len):
    t = x2d.shape[0]
    tm = TM_SSD
    n_tiles = t // tm
    cur = lambda width: pl.BlockSpec((tm, width), lambda i: (jnp.minimum(i, n_tiles - 1), 0))
    prev = lambda width: pl.BlockSpec((tm, width), lambda i: (jnp.maximum(i - 1, 0), 0))
    full = lambda r, width: pl.BlockSpec((r, width), lambda i: (0, 0))
    part = lambda r, width: pl.BlockSpec((r, width), lambda i: (jnp.minimum(i, n_tiles - 1), 0))
    rd = D_FF // n_tiles
    return pl.pallas_call(
        functools.partial(_ssd_outproj_kernel, tiles_per_seq=seq_len // tm),
        grid=(n_tiles + 1,),
        in_specs=[cur(SSM_CONV_DIM), cur(SSM_D_INNER), cur(DT_PAD),
                  full(SSM_CONV, SSM_CONV_DIM), full(1, SSM_CONV_DIM),
                  full(1, DT_PAD), full(1, DT_PAD),
                  full(1, SSM_D_INNER), full(1, SSM_D_INNER),
                  prev(D_MODEL), prev(ATTN_WIDTH),
                  pl.BlockSpec((D_MODEL, D_MODEL), lambda i: (0, 0), pipeline_mode=pl.Buffered(1)),
                  full(1, D_MODEL),
                  part(rd, D_MODEL)],
        out_specs=[prev(D_MODEL), prev(D_MODEL), part(rd, D_MODEL)],
        out_shape=[jax.ShapeDtypeStruct((t, D_MODEL), F32),
                   jax.ShapeDtypeStruct((t, D_MODEL), BF16),
                   jax.ShapeDtypeStruct((D_FF, D_MODEL), BF16)],
        scratch_shapes=[pltpu.VMEM((SUBLANES + SSM_CHUNK, SSM_CONV_DIM), F32),
                        pltpu.VMEM((SSM_D_STATE, SSM_D_INNER), F32),
                        pltpu.VMEM((tm, SSM_D_INNER), BF16),
                        pltpu.VMEM((tm, SSM_D_INNER), BF16)],
        compiler_params=pltpu.CompilerParams(
            dimension_semantics=("arbitrary",), vmem_limit_bytes=VMEM_LIMIT),
        name="ssd_outproj",
    )(xbc, z, dt, conv_w, conv_b, dt_bias, a_log, d_skip, norm_g,
      x2d, y_att, w_out16, mlp_g, w_down)


def _mlp_kernel(x1_ref, h2_ref, wu_ref, wd_ref, g_ref, o_ref):
    j = pl.program_id(1)
    last = pl.num_programs(1) - 1

    def step(first, final):
        u = jnp.maximum(_dot(h2_ref[...], wu_ref[...]), 0.0)
        base = x1_ref[...] if first else o_ref[...]
        x2 = base + _dot((u * u).astype(BF16), wd_ref[...])
        o_ref[...] = x2 * _rms_scale(x2) * g_ref[...] if final else x2

    @pl.when(j == 0)
    def _():
        step(True, False)

    @pl.when((j > 0) & (j < last))
    def _():
        step(False, False)

    @pl.when(j == last)
    def _():
        step(False, True)


def _mlp(x1, h2, w_up, w_down, g):
    t = x1.shape[0]
    tm, tf = TM_MLP, TF_MLP
    assert D_FF // tf >= 2, "the kernel treats the first and last d_ff blocks separately"
    return pl.pallas_call(
        _mlp_kernel,
        grid=(t // tm, D_FF // tf),
        in_specs=[pl.BlockSpec((tm, D_MODEL), lambda i, j: (i, 0)),
                  pl.BlockSpec((tm, D_MODEL), lambda i, j: (i, 0)),
                  pl.BlockSpec((D_MODEL, tf), lambda i, j: (0, j)),
                  pl.BlockSpec((tf, D_MODEL), lambda i, j: (j, 0)),
                  pl.BlockSpec((1, D_MODEL), lambda i, j: (0, 0))],
        out_specs=pl.BlockSpec((tm, D_MODEL), lambda i, j: (i, 0)),
        out_shape=jax.ShapeDtypeStruct((t, D_MODEL), F32),
        compiler_params=pltpu.CompilerParams(
            dimension_semantics=("parallel", "arbitrary"), vmem_limit_bytes=VMEM_LIMIT_BIG),
        name="mlp",
    )(x1, h2, w_up, w_down, g)


def _row(v, width=None):
    v = v.astype(F32).reshape(1, -1)
    if width is not None and v.shape[1] < width:
        v = jnp.pad(v, ((0, 0), (0, width - v.shape[1])))
    return v


def kernel(x, mix_norm_g, w_in, conv_w, conv_b, dt_bias, A_log, D_skip, ssm_norm_g,
           attn_sinks, attn_out_norm_g, w_out, mlp_norm_g, w_up, w_down, final_norm_g):
    b, l, d = x.shape
    assert d == D_MODEL and l % SSM_CHUNK == 0 and l % TM_PROJ == 0
    assert w_in.shape[0] == 1, "one layer"
    x2d = x.reshape(b * l, d)

    z, xbc, dt, y_att, w_out16, w_up16 = _inproj_attn(
        x2d, _row(mix_norm_g[0]), jnp.swapaxes(w_in.astype(F32), 1, 2),
        attn_sinks[0].astype(F32), _row(attn_out_norm_g[0]), w_out[0], w_up[0], l)

    x1, h2, w_down16 = _ssd_outproj(
        x2d, xbc, z, dt, y_att,
        conv_w[0].astype(F32), _row(conv_b[0]),
        _row(dt_bias[0], DT_PAD), _row(A_log[0], DT_PAD),
        _row(jnp.repeat(D_skip[0], SSM_HEAD_DIM)), _row(ssm_norm_g[0]),
        w_out16, _row(mlp_norm_g[0]), w_down[0], l)
    out = _mlp(x1, h2, w_up16, w_down16, _row(final_norm_g))
    return out.reshape(b, l, d)
```

```python
import functools

import jax
import jax.numpy as jnp
from jax import lax
from jax.experimental import pallas as pl
from jax.experimental.pallas import tpu as pltpu

F32 = jnp.float32
BF16 = jnp.bfloat16

D_MODEL = 2048
SSM_D_INNER = 1024
SSM_HEAD_DIM = 64
SSM_N_HEADS = 16
SSM_N_GROUPS = 4
SSM_D_STATE = 128
SSM_CONV = 4
SSM_CHUNK = 128
SSM_GN = SSM_N_GROUPS * SSM_D_STATE
SSM_CONV_DIM = SSM_D_INNER + 2 * SSM_GN
SSM_GROUP_WIDTH = SSM_D_INNER // SSM_N_GROUPS
ATTN_WIDTH = 1024
ATTN_HEAD_DIM = 64
ATTN_N_HEADS = 16
ATTN_N_KV = 2
ATTN_KV_WIDTH = 2 * ATTN_N_KV * ATTN_HEAD_DIM
ATTN_BLOCK = 128
D_FF = 8192
EPS = 1e-5
LOG2E = 1.4426950408889634

LANES = 128
DT_PAD = LANES
SUBLANES = 8
MXU_WIDTH = 256
PROJ_ATTN_PATTERN = "abbbbc"
VMEM_LIMIT = 56 * 1024 * 1024
VMEM_LIMIT_BIG = 62 * 1024 * 1024

SCAN_PROJ_PATTERN = "b" "abaababaabaa" "cbccbcbccbcc" "wb"

TM_PROJ = 512
W_STAGE_ROWS = 208
TM_SSD = 256
X_RING = 3
TM_MLP = 512
TF_MLP = 2048


def _dot(a, b):
    return jnp.dot(a, b, preferred_element_type=F32)


def _split3(x):
    hi = x.astype(BF16)
    r = x - hi.astype(F32)
    mid = r.astype(BF16)
    lo = (r - mid.astype(F32)).astype(BF16)
    return hi, mid, lo


def _rms_scale(x):
    return lax.rsqrt(jnp.mean(x * x, axis=-1, keepdims=True) + EPS)


def _silu(x):
    hx = 0.5 * x
    return hx + hx * jnp.tanh(hx)


def _interleave(gens, pattern):
    live = {key: True for key in gens}
    while any(live.values()):
        for key in pattern:
            if live[key]:
                live[key] = next(gens[key], _DONE) is not _DONE


_DONE = object()


def _attn_block_stages(q_blk, kv_prev, kv_cur, first_block, sink_ref, emit):
    QB = ATTN_BLOCK
    D = ATTN_HEAD_DIM
    kv = jnp.concatenate([kv_prev, kv_cur], axis=0).astype(F32)
    k01 = kv[:, :2 * D]
    v01 = kv[:, 2 * D:]
    k_t = k01.T
    v01r = pltpu.roll(v01, D, axis=1)

    lane_kv = lax.broadcasted_iota(jnp.int32, (2 * QB, LANES), 1)
    first_kv = lane_kv < D
    ones_e = first_kv.astype(BF16)
    ones_o = 1 - ones_e
    zero_t = jnp.zeros((D, 2 * QB), BF16)

    i = lax.broadcasted_iota(jnp.int32, (QB, 2 * QB), 0)
    j = lax.broadcasted_iota(jnp.int32, (QB, 2 * QB), 1)
    valid = (j > i) & (j <= i + QB) & ((j >= QB) | jnp.logical_not(first_block))
    lane_q = lax.broadcasted_iota(jnp.int32, (QB, LANES), 1)
    first_q = lane_q < D

    per_kv = ATTN_N_HEADS // ATTN_N_KV // 2
    rhs, vb = [], []
    for g in range(ATTN_N_KV):
        kg_t = k_t[g * D:(g + 1) * D, :].astype(BF16)
        rhs.append(jnp.concatenate([jnp.concatenate([kg_t, zero_t], axis=1),
                                    jnp.concatenate([zero_t, kg_t], axis=1)], axis=0))
        if g == 0:
            v_e = jnp.where(first_kv, v01, 0.0)
            v_o = jnp.where(first_kv, 0.0, v01r)
        else:
            v_e = jnp.where(first_kv, v01r, 0.0)
            v_o = jnp.where(first_kv, 0.0, v01)
        vb.append(jnp.concatenate(
            [jnp.concatenate([v_e.astype(BF16), ones_e], axis=1),
             jnp.concatenate([v_o.astype(BF16), ones_o], axis=1)], axis=0))
    yield

    def scores(jp):
        s = _dot(q_blk(jp), rhs[jp // per_kv]) * (D ** -0.5)
        s_e = jnp.where(valid, s[:, :2 * QB], -jnp.inf)
        s_o = jnp.where(valid, s[:, 2 * QB:], -jnp.inf)
        sink_e = sink_ref[2 * jp]
        sink_o = sink_ref[2 * jp + 1]
        m_e = jnp.maximum(jnp.max(s_e, axis=-1, keepdims=True), sink_e)
        m_o = jnp.maximum(jnp.max(s_o, axis=-1, keepdims=True), sink_o)
        p = jnp.concatenate([jnp.exp(s_e - m_e), jnp.exp(s_o - m_o)], axis=1).astype(BF16)
        sink_term = jnp.where(first_q, jnp.exp(sink_e - m_e), jnp.exp(sink_o - m_o))
        return p, sink_term

    def values(jp, p, sink_term):
        o = _dot(p, vb[jp // per_kv])
        return o[:, :LANES] / (o[:, LANES:] + sink_term)

    n_pairs = ATTN_N_KV * per_kv
    outs = []
    pending = scores(0)
    yield
    for jp in range(1, n_pairs):
        nxt = scores(jp)
        yield
        outs.append(values(jp - 1, *pending))
        pending = nxt
        yield
    outs.append(values(n_pairs - 1, *pending))
    emit(jnp.concatenate(outs, axis=1))
    yield


def _inproj_attn_kernel(sink_ref, x_ref, g_ref, wt_hbm, ng_ref, wo_ref, wu_ref,
                        z_ref, xbc_ref, dt_ref, yatt_ref, wo16_ref, wu16_ref,
                        wt_ref, cur_scr, prev_scr, kvtail_scr, stage, stage_sem,
                        *, tiles_per_seq):
    s = pl.program_id(0)
    n_tiles = pl.num_programs(0) - 1
    blocks = TM_PROJ // ATTN_BLOCK
    tail = slice(TM_PROJ - ATTN_BLOCK, TM_PROJ)

    def cast_weights():
        wo16_ref[...] = wo_ref[...].astype(BF16)
        yield
        for lo in range(0, D_FF, D_MODEL):
            wu16_ref[:, lo:lo + D_MODEL] = wu_ref[:, lo:lo + D_MODEL].astype(BF16)
            yield

    def stage_copy(r, slot):
        return pltpu.make_async_copy(wt_hbm.at[0, pl.ds(r * W_STAGE_ROWS, W_STAGE_ROWS), :],
                                     stage.at[slot], stage_sem.at[slot])

    def load_weights():
        n_chunks = wt_ref.shape[0] // W_STAGE_ROWS
        stage_copy(0, 0).start()
        for r in range(n_chunks):
            slot = r % 2
            if r + 1 < n_chunks:
                stage_copy(r + 1, 1 - slot).start()
            stage_copy(r, slot).wait()
            wt_ref[r * W_STAGE_ROWS:(r + 1) * W_STAGE_ROWS, :] = stage[slot].astype(BF16)

    def project():
        x = x_ref[...]
        h = (x * _rms_scale(x) * g_ref[...]).astype(BF16)
        yield

        def cols(lo, width):
            return lax.dot_general(h, wt_ref[lo:lo + width, :], (((1,), (1,)), ((), ())),
                                   preferred_element_type=F32)

        o_dt = SSM_D_INNER + SSM_CONV_DIM
        o_q = o_dt + SSM_N_HEADS
        for dst, base, width in ((z_ref, 0, SSM_D_INNER), (xbc_ref, SSM_D_INNER, SSM_CONV_DIM),
                                 (cur_scr, o_q, ATTN_WIDTH + ATTN_KV_WIDTH)):
            for lo in range(0, width, MXU_WIDTH):
                dst[:, lo:lo + MXU_WIDTH] = cols(base + lo, MXU_WIDTH).astype(BF16)
                yield
        lane = lax.broadcasted_iota(jnp.int32, (TM_PROJ, DT_PAD), 1)
        dt_ref[...] = jnp.where(lane < SSM_N_HEADS, cols(o_dt, DT_PAD), 0.0)
        yield

    def rotate_buffers():
        kvtail_scr[...] = prev_scr[tail, ATTN_WIDTH:]
        prev_scr[...] = cur_scr[...]

    def attend():
        seq_start = ((s - 1) % tiles_per_seq) == 0
        for r in range(blocks):
            lo = r * ATTN_BLOCK
            kv_cur = prev_scr[lo:lo + ATTN_BLOCK, ATTN_WIDTH:]
            if r == 0:
                kv_prev = kvtail_scr[...]
                first_block = seq_start
            else:
                kv_prev = prev_scr[lo - ATTN_BLOCK:lo, ATTN_WIDTH:]
                first_block = False
            q_blk = lambda jp, lo=lo: prev_scr[lo:lo + ATTN_BLOCK, jp * LANES:(jp + 1) * LANES]

            def emit(y, lo=lo):
                yatt_ref[lo:lo + ATTN_BLOCK, :] = (y * _rms_scale(y) * ng_ref[...]).astype(BF16)

            yield from _attn_block_stages(q_blk, kv_prev, kv_cur, first_block, sink_ref, emit)

    @pl.when(s == 0)
    def _():
        prev_scr[tail, ATTN_WIDTH:] = jnp.zeros((ATTN_BLOCK, ATTN_KV_WIDTH), BF16)
        load_weights()
        _interleave({"a": project(), "c": cast_weights()}, "ac")

    @pl.when((s > 0) & (s < n_tiles))
    def _():
        rotate_buffers()
        _interleave({"a": project(), "b": attend(), "c": cast_weights()}, PROJ_ATTN_PATTERN)

    @pl.when(s == n_tiles)
    def _():
        rotate_buffers()
        for _ in attend():
            pass


def _inproj_attn(x2d, g, w_in_t, sinks, attn_g, w_out, w_up, seq_len):
    t = x2d.shape[0]
    n_cols = w_in_t.shape[1]
    assert n_cols % W_STAGE_ROWS == 0 and w_in_t.shape[2] == D_MODEL
    tm = TM_PROJ
    n_tiles = t // tm
    cur = lambda width: pl.BlockSpec((tm, width), lambda i: (jnp.minimum(i, n_tiles - 1), 0))
    prev = lambda width: pl.BlockSpec((tm, width), lambda i: (jnp.maximum(i - 1, 0), 0))
    vec = lambda width: pl.BlockSpec((1, width), lambda i: (0, 0))
    w_rows = lambda width: pl.BlockSpec((D_MODEL // n_tiles, width),
                                        lambda i: (jnp.minimum(i, n_tiles - 1), 0))
    return pl.pallas_call(
        functools.partial(_inproj_attn_kernel, tiles_per_seq=seq_len // tm),
        grid=(n_tiles + 1,),
        in_specs=[
            pl.BlockSpec(memory_space=pltpu.SMEM),
            cur(D_MODEL),
            vec(D_MODEL),
            pl.BlockSpec(memory_space=pl.ANY),
            vec(ATTN_WIDTH),
            w_rows(D_MODEL),
            w_rows(D_FF),
        ],
        out_specs=[cur(SSM_D_INNER), cur(SSM_CONV_DIM), cur(DT_PAD), prev(ATTN_WIDTH),
                   w_rows(D_MODEL), w_rows(D_FF)],
        out_shape=[
            jax.ShapeDtypeStruct((t, SSM_D_INNER), BF16),
            jax.ShapeDtypeStruct((t, SSM_CONV_DIM), BF16),
            jax.ShapeDtypeStruct((t, DT_PAD), F32),
            jax.ShapeDtypeStruct((t, ATTN_WIDTH), BF16),
            jax.ShapeDtypeStruct((D_MODEL, D_MODEL), BF16),
            jax.ShapeDtypeStruct((D_MODEL, D_FF), BF16),
        ],
        scratch_shapes=[pltpu.VMEM((n_cols, D_MODEL), BF16),
                        pltpu.VMEM((tm, ATTN_WIDTH + ATTN_KV_WIDTH), BF16),
                        pltpu.VMEM((tm, ATTN_WIDTH + ATTN_KV_WIDTH), BF16),
                        pltpu.VMEM((ATTN_BLOCK, ATTN_KV_WIDTH), BF16),
                        pltpu.VMEM((2, W_STAGE_ROWS, D_MODEL), F32),
                        pltpu.SemaphoreType.DMA((2,))],
        compiler_params=pltpu.CompilerParams(
            dimension_semantics=("arbitrary",), vmem_limit_bytes=VMEM_LIMIT_BIG),
        name="inproj_attn",
    )(sinks, x2d, g, w_in_t, attn_g, w_out, w_up)


def _ssd_chunk_stages(xbc, z, dt_raw, cw_ref, cb_ref, dtb_ref, alog_ref, dskip_ref, ng_ref,
                      cbuf, state, emit):
    Q = SSM_CHUNK
    heads_per_group = SSM_N_HEADS // SSM_N_GROUPS
    dtr = dt_raw + dtb_ref[...]
    dt = jnp.maximum(dtr, 0.0) + jnp.log1p(jnp.exp(-jnp.abs(dtr)))
    a = dt * (-LOG2E * jnp.exp(alog_ref[...]))

    row = lax.broadcasted_iota(jnp.int32, (Q, Q), 0)
    col = lax.broadcasted_iota(jnp.int32, (Q, Q), 1)
    causal = row >= col
    tri = causal.astype(BF16)
    a_hi, a_mid, a_lo = _split3(a)
    a_cs = _dot(tri, a_hi) + _dot(tri, a_mid) + _dot(tri, a_lo)
    a_cs_t = a_cs.T
    yield

    cur = xbc.astype(F32)
    cbuf[SUBLANES:SUBLANES + Q, :] = cur
    acc = cb_ref[...] + cw_ref[SSM_CONV - 1:SSM_CONV, :] * cur
    for k in range(SSM_CONV - 1):
        lo = SUBLANES - (SSM_CONV - 1) + k
        acc = acc + cw_ref[k:k + 1, :] * cbuf[lo:lo + Q, :]
    cbuf[0:SUBLANES, :] = cur[Q - SUBLANES:Q, :]
    xc = _silu(acc)
    xs = xc[:, :SSM_D_INNER]
    yield

    er = lax.broadcasted_iota(jnp.int32, (LANES, SSM_D_INNER), 0)
    ec = lax.broadcasted_iota(jnp.int32, (LANES, SSM_D_INNER), 1)
    expand = ((ec // SSM_HEAD_DIM) == er).astype(BF16)
    stacked = jnp.concatenate([dt, a_cs], axis=0)
    s_hi, s_mid, _ = _split3(stacked)
    ex = _dot(s_hi, expand) + _dot(s_mid, expand)
    dt_x = ex[:Q]
    acs_x = ex[Q:]
    acs_last = acs_x[Q - 1:Q, :]
    yield

    xd = xs * dt_x
    xd16 = xd.astype(BF16)
    xdd16 = (xd * jnp.exp2(acs_last - acs_x)).astype(BF16)
    exp_acs = jnp.exp2(acs_x)
    chunk_decay = jnp.exp2(acs_last)

    lane = lax.broadcasted_iota(jnp.int32, (Q, LANES), 1)
    first_half = lane < SSM_HEAD_DIM
    zero16 = jnp.zeros((Q, LANES), BF16)

    b16, c16, cb = [], [], []
    for g in range(SSM_N_GROUPS):
        b16.append(xc[:, SSM_D_INNER + g * SSM_D_STATE:
                      SSM_D_INNER + (g + 1) * SSM_D_STATE].astype(BF16))
        c16.append(xc[:, SSM_D_INNER + SSM_GN + g * SSM_D_STATE:
                      SSM_D_INNER + SSM_GN + (g + 1) * SSM_D_STATE].astype(BF16))
        cb.append(lax.dot_general(c16[g], b16[g], (((1,), (1,)), ((), ())),
                                  preferred_element_type=F32))
    yield

    y_off = []
    for g in range(SSM_N_GROUPS):
        gl = g * SSM_GROUP_WIDTH
        st = state[:, gl:gl + SSM_GROUP_WIDTH]
        y_off.append(_dot(c16[g], st.astype(BF16)) * exp_acs[:, gl:gl + SSM_GROUP_WIDTH])
        upd = lax.dot_general(b16[g], xdd16[:, gl:gl + SSM_GROUP_WIDTH],
                              (((0,), (0,)), ((), ())), preferred_element_type=F32)
        state[:, gl:gl + SSM_GROUP_WIDTH] = chunk_decay[:, gl:gl + SSM_GROUP_WIDTH] * st + upd
    yield

    y_parts = []
    for g in range(SSM_N_GROUPS):
        ms = []
        for r in range(heads_per_group):
            h = g * heads_per_group + r
            seg = a_cs[:, h:h + 1] - a_cs_t[h:h + 1, :]
            lmat = jnp.exp2(jnp.where(causal, seg, -jnp.inf))
            ms.append((cb[g] * lmat).astype(BF16))
        gl = g * SSM_GROUP_WIDTH
        yd = []
        for pr in range(2):
            xp = xd16[:, gl + pr * LANES:gl + (pr + 1) * LANES]
            rhs = jnp.concatenate([jnp.where(first_half, xp, zero16),
                                   jnp.where(first_half, zero16, xp)], axis=0)
            lhs = jnp.concatenate([ms[2 * pr], ms[2 * pr + 1]], axis=1)
            yd.append(_dot(lhs, rhs))
        y_parts.append(jnp.concatenate(yd, axis=1) + y_off[g])
        if g % 2 == 1:
            yield

    y = jnp.concatenate(y_parts, axis=1) + xs * dskip_ref[...]
    y = y * _silu(z.astype(F32))
    outs = []
    for g in range(SSM_N_GROUPS):
        yg = y[:, g * SSM_GROUP_WIDTH:(g + 1) * SSM_GROUP_WIDTH]
        outs.append(yg * _rms_scale(yg))
    emit(jnp.concatenate(outs, axis=1) * ng_ref[...])
    yield


def _ssd_outproj_kernel(xbc_ref, z_ref, dt_ref, cw_ref, cb_ref, dtb_ref, alog_ref, dskip_ref,
                        ng_ref, x_hbm, ya_ref, wo_ref, g2_ref, wd_ref,
                        x1_ref, h2_ref, wd16_ref,
                        cbuf, state, ys_cur, ys_prev, xring, xsem, *, tiles_per_seq):
    s = pl.program_id(0)
    n_tiles = pl.num_programs(0) - 1
    Q = SSM_CHUNK

    def cast_weights():
        wd16_ref[...] = wd_ref[...].astype(BF16)
        yield

    def x_copy(t, slot):
        return pltpu.make_async_copy(x_hbm.at[pl.ds(pl.multiple_of(t * TM_SSD, TM_SSD), TM_SSD), :],
                                     xring.at[slot], xsem.at[slot])

    @pl.when(s == 0)
    def _():
        x_copy(0, 0).start()
        x_copy(1, 1).start()

    @pl.when(s >= 1)
    def _():
        x_copy(s - 1, (s - 1) % X_RING).wait()

    @pl.when((s >= 1) & (s <= n_tiles - 2))
    def _():
        x_copy(s + 1, (s + 1) % X_RING).start()

    x_slot = (s - 1) % X_RING

    @pl.when((s % tiles_per_seq) == 0)
    def _():
        cbuf[0:SUBLANES, :] = jnp.zeros((SUBLANES, SSM_CONV_DIM), F32)
        state[...] = jnp.zeros_like(state)

    def chunk(c):
        lo = c * Q

        def emit(y):
            ys_cur[lo:lo + Q, :] = y.astype(BF16)

        return _ssd_chunk_stages(
            xbc_ref[lo:lo + Q, :], z_ref[lo:lo + Q, :], dt_ref[lo:lo + Q, :],
            cw_ref, cb_ref, dtb_ref, alog_ref, dskip_ref, ng_ref, cbuf, state, emit)

    def project():
        y = jnp.concatenate([ys_prev[...], ya_ref[...]], axis=1)
        yield
        for lo in range(0, D_MODEL, MXU_WIDTH):
            x1_ref[:, lo:lo + MXU_WIDTH] = (xring[x_slot, :, lo:lo + MXU_WIDTH]
                                            + _dot(y, wo_ref[:, lo:lo + MXU_WIDTH]))
            yield
        x1 = x1_ref[...]
        h2_ref[...] = (x1 * _rms_scale(x1) * g2_ref[...]).astype(BF16)
        yield

    assert TM_SSD == 2 * Q, "the trace patterns below are written for two chunks per tile"

    @pl.when(s == 0)
    def _():
        _interleave({"a": chunk(0), "c": chunk(1), "w": cast_weights()}, 8 * "a" + 8 * "c" + "w")

    @pl.when((s > 0) & (s < n_tiles))
    def _():
        ys_prev[...] = ys_cur[...]
        _interleave({"a": chunk(0), "c": chunk(1), "b": project(), "w": cast_weights()},
                    SCAN_PROJ_PATTERN)

    @pl.when(s == n_tiles)
    def _():
        ys_prev[...] = ys_cur[...]
        for _ in project():
            pass


def _ssd_outproj(x2d, xbc, z, dt, y_att, conv_w, conv_b, dt_bias, a_log, d_skip, norm_g,
                 w_out16, mlp_g, w_down, seq_len):
    t = x2d.shape[0]
    tm = TM_SSD
    n_tiles = t // tm
    cur = lambda width: pl.BlockSpec((tm, width), lambda i: (jnp.minimum(i, n_tiles - 1), 0))
    prev = lambda width: pl.BlockSpec((tm, width), lambda i: (jnp.maximum(i - 1, 0), 0))
    full = lambda r, width: pl.BlockSpec((r, width), lambda i: (0, 0))
    part = lambda r, width: pl.BlockSpec((r, width), lambda i: (jnp.minimum(i, n_tiles - 1), 0))
    rd = D_FF // n_tiles
    return pl.pallas_call(
        functools.partial(_ssd_outproj_kernel, tiles_per_seq=seq_len // tm),
        grid=(n_tiles + 1,),
        in_specs=[cur(SSM_CONV_DIM), cur(SSM_D_INNER), cur(DT_PAD),
                  full(SSM_CONV, SSM_CONV_DIM), full(1, SSM_CONV_DIM),
                  full(1, DT_PAD), full(1, DT_PAD),
                  full(1, SSM_D_INNER), full(1, SSM_D_INNER),
                  pl.BlockSpec(memory_space=pl.ANY), prev(ATTN_WIDTH),
                  pl.BlockSpec((D_MODEL, D_MODEL), lambda i: (0, 0), pipeline_mode=pl.Buffered(1)),
                  full(1, D_MODEL),
                  part(rd, D_MODEL)],
        out_specs=[prev(D_MODEL), prev(D_MODEL), part(rd, D_MODEL)],
        out_shape=[jax.ShapeDtypeStruct((t, D_MODEL), F32),
                   jax.ShapeDtypeStruct((t, D_MODEL), BF16),
                   jax.ShapeDtypeStruct((D_FF, D_MODEL), BF16)],
        scratch_shapes=[pltpu.VMEM((SUBLANES + SSM_CHUNK, SSM_CONV_DIM), F32),
                        pltpu.VMEM((SSM_D_STATE, SSM_D_INNER), F32),
                        pltpu.VMEM((tm, SSM_D_INNER), BF16),
                        pltpu.VMEM((tm, SSM_D_INNER), BF16),
                        pltpu.VMEM((X_RING, tm, D_MODEL), F32),
                        pltpu.SemaphoreType.DMA((X_RING,))],
        compiler_params=pltpu.CompilerParams(
            dimension_semantics=("arbitrary",), vmem_limit_bytes=VMEM_LIMIT),
        name="ssd_outproj",
    )(xbc, z, dt, conv_w, conv_b, dt_bias, a_log, d_skip, norm_g,
      x2d, y_att, w_out16, mlp_g, w_down)


def _mlp_kernel(x1_ref, h2_ref, wu_ref, wd_ref, g_ref, o_ref):
    j = pl.program_id(1)
    last = pl.num_programs(1) - 1

    def step(first, final):
        u = jnp.maximum(_dot(h2_ref[...], wu_ref[...]), 0.0)
        base = x1_ref[...] if first else o_ref[...]
        x2 = base + _dot((u * u).astype(BF16), wd_ref[...])
        o_ref[...] = x2 * _rms_scale(x2) * g_ref[...] if final else x2

    @pl.when(j == 0)
    def _():
        step(True, False)

    @pl.when((j > 0) & (j < last))
    def _():
        step(False, False)

    @pl.when(j == last)
    def _():
        step(False, True)


def _mlp(x1, h2, w_up, w_down, g):
    t = x1.shape[0]
    tm, tf = TM_MLP, TF_MLP
    assert D_FF // tf >= 2, "the kernel treats the first and last d_ff blocks separately"
    return pl.pallas_call(
        _mlp_kernel,
        grid=(t // tm, D_FF // tf),
        in_specs=[pl.BlockSpec((tm, D_MODEL), lambda i, j: (i, 0)),
                  pl.BlockSpec((tm, D_MODEL), lambda i, j: (i, 0)),
                  pl.BlockSpec((D_MODEL, tf), lambda i, j: (0, j)),
                  pl.BlockSpec((tf, D_MODEL), lambda i, j: (j, 0)),
                  pl.BlockSpec((1, D_MODEL), lambda i, j: (0, 0))],
        out_specs=pl.BlockSpec((tm, D_MODEL), lambda i, j: (i, 0)),
        out_shape=jax.ShapeDtypeStruct((t, D_MODEL), F32),
        compiler_params=pltpu.CompilerParams(
            dimension_semantics=("parallel", "arbitrary"), vmem_limit_bytes=VMEM_LIMIT_BIG),
        name="mlp",
    )(x1, h2, w_up, w_down, g)


def _row(v, width=None):
    v = v.astype(F32).reshape(1, -1)
    if width is not None and v.shape[1] < width:
        v = jnp.pad(v, ((0, 0), (0, width - v.shape[1])))
    return v


def kernel(x, mix_norm_g, w_in, conv_w, conv_b, dt_bias, A_log, D_skip, ssm_norm_g,
           attn_sinks, attn_out_norm_g, w_out, mlp_norm_g, w_up, w_down, final_norm_g):
    b, l, d = x.shape
    assert d == D_MODEL and l % SSM_CHUNK == 0 and l % TM_PROJ == 0
    assert w_in.shape[0] == 1, "one layer"
    x2d = x.reshape(b * l, d)

    z, xbc, dt, y_att, w_out16, w_up16 = _inproj_attn(
        x2d, _row(mix_norm_g[0]), jnp.swapaxes(w_in.astype(F32), 1, 2),
        attn_sinks[0].astype(F32), _row(attn_out_norm_g[0]), w_out[0], w_up[0], l)

    x1, h2, w_down16 = _ssd_outproj(
        x2d, xbc, z, dt, y_att,
        conv_w[0].astype(F32), _row(conv_b[0]),
        _row(dt_bias[0], DT_PAD), _row(A_log[0], DT_PAD),
        _row(jnp.repeat(D_skip[0], SSM_HEAD_DIM)), _row(ssm_norm_g[0]),
        w_out16, _row(mlp_norm_g[0]), w_down[0], l)
    out = _mlp(x1, h2, w_up16, w_down16, _row(final_norm_g))
    return out.reshape(b, l, d)
```
